```python
import math
import jax, jax.numpy as jnp
from jax import lax
import numpy as np

D_MODEL = 4096
BATCH = 32
SEQ = 256
DEPTH = 4
DEC_BATCH = 2
DEC_SEQ = 4096
PAST_LEN = 512

GRID_W = 64
N_MIXERS = 2
N_DN = (DEPTH + 1) // 2
N_HY = DEPTH // 2
DN_HEADS = 32
DN_HEAD_DIM = D_MODEL // DN_HEADS
DN_CONV = 5
DN_CHUNK = 64
HY_WIDTH = D_MODEL
HY_SHORT = 3
HY_BANDS = 16
HY_POS_DIM = 2 * HY_BANDS + 1
HY_HIDDEN = 64
HY_DECAY_FAST = -math.log(1e-2) / 0.3
HY_DECAY_SLOW = -math.log(1e-2) / 1.5
N_EXPERTS = 16
EC_FACTOR = 2
EXPERT_FF = D_MODEL // 4
N_MOD = 6
EPS = 1e-6

kernel_name = "hybrid_deltanet_hyena_ec_moe_diffusion_step"


def rms_norm(x, g):
    xf = x.astype(jnp.float32)
    y = xf * lax.rsqrt(jnp.mean(xf * xf, axis=-1, keepdims=True) + EPS)
    return (y * g.astype(jnp.float32)).astype(x.dtype)


def l2_normalize(t):
    return t * lax.rsqrt(jnp.sum(t * t, axis=-1, keepdims=True) + EPS)


def row_conv(x, w, n_rows):
    b, l, ch = x.shape
    k = w.shape[-1]
    p = k // 2
    seg = l // n_rows
    xp = jnp.pad(x.reshape(b, n_rows, seg, ch), ((0, 0), (0, 0), (p, p), (0, 0)))
    y = xp[:, :, 0:seg, :] * w[:, 0]
    for j in range(1, k):
        y = y + xp[:, :, j:j + seg, :] * w[:, j]
    return y.reshape(b, l, ch)


def ada_mod(cvec, w, b):
    m = jax.nn.silu(cvec) @ w + b
    return [t[:, None, :] for t in jnp.split(m, N_MOD, axis=-1)]


def gated_delta_chunked(q, k, v, g, beta, s0):
    b, h, l, dk = k.shape
    dv = v.shape[-1]
    c = DN_CHUNK
    n = l // c
    q, k, v = (t.reshape(b, h, n, c, t.shape[-1]) for t in (q, k, v))
    g, beta = (t.reshape(b, h, n, c) for t in (g, beta))
    g = jnp.cumsum(g, axis=-1)
    incl = jnp.tril(jnp.ones((c, c), dtype=bool))
    strict = jnp.tril(jnp.ones((c, c), dtype=bool), -1)
    decay = jnp.exp(jnp.where(incl, g[..., :, None] - g[..., None, :], -jnp.inf))
    kb = k * beta[..., None]
    lmat = jnp.where(strict, jnp.einsum('bhncd,bhnsd->bhncs', kb, k) * decay, 0.0)
    tmat = jnp.eye(c, dtype=jnp.float32) + lmat
    rhs = jnp.concatenate([v * beta[..., None], kb * jnp.exp(g)[..., None]], axis=-1)
    sol = lax.linalg.triangular_solve(tmat, rhs, left_side=True, lower=True, unit_diagonal=True)
    u, w = sol[..., :dv], sol[..., dv:]
    attn = jnp.einsum('bhncd,bhnsd->bhncs', q, k) * decay

    def step(s, xs):
        qc, kc, uc, wc, gc, ac = xs
        v_new = uc - jnp.einsum('bhck,bhkv->bhcv', wc, s)
        o = (jnp.einsum('bhck,bhkv->bhcv', qc * jnp.exp(gc)[..., None], s)
             + jnp.einsum('bhcs,bhsv->bhcv', ac, v_new))
        g_last = gc[..., -1:]
        s = (s * jnp.exp(g_last)[..., None]
             + jnp.einsum('bhck,bhcv->bhkv', kc * jnp.exp(g_last - gc)[..., None], v_new))
        return s, o

    xs = tuple(jnp.moveaxis(t, 2, 0) for t in (q, k, u, w, g, attn))
    s_fin, o = lax.scan(step, s0.astype(jnp.float32), xs)
    return jnp.moveaxis(o, 0, 2).reshape(b, h, l, dv), s_fin


def deltanet_mixer(hn, n_rows, s0, w_in, w_conv, a_log, dt_bias, norm_g, w_out):
    b, l, d = hn.shape
    proj = hn @ w_in
    qkv, z, gates = jnp.split(proj, [3 * d, 4 * d], axis=-1)
    qkv = jax.nn.silu(row_conv(qkv, w_conv, n_rows)).astype(jnp.float32)

    def heads(t):
        return t.reshape(b, l, DN_HEADS, DN_HEAD_DIM).transpose(0, 2, 1, 3)

    q, k, v = (heads(t) for t in jnp.split(qkv, 3, axis=-1))
    q = l2_normalize(q) * (DN_HEAD_DIM ** -0.5)
    k = l2_normalize(k)
    gates = gates.astype(jnp.float32).reshape(b, l, 2, 2, DN_HEADS)
    beta = jax.nn.sigmoid(gates[:, :, 0]).transpose(0, 2, 3, 1)
    g = (-jnp.exp(a_log.astype(jnp.float32))
         * jax.nn.softplus(gates[:, :, 1] + dt_bias)).transpose(0, 2, 3, 1)
    o_f, s_f = gated_delta_chunked(q, k, v, g[:, 0], beta[:, 0], s0[:, 0])

    def rev(t):
        return jnp.flip(t, axis=2)

    o_b, s_b = gated_delta_chunked(rev(q), rev(k), rev(v), rev(g[:, 1]), rev(beta[:, 1]), s0[:, 1])
    o = o_f + rev(o_b)
    o = o * lax.rsqrt(jnp.mean(o * o, axis=-1, keepdims=True) + EPS) * norm_g.astype(jnp.float32)
    o = o.transpose(0, 2, 1, 3).reshape(b, l, d) * jax.nn.silu(z.astype(jnp.float32))
    return o.astype(hn.dtype) @ w_out, jnp.stack([s_f, s_b], axis=1)


def hyena_filter(l, w1, b1, w2, b2, w3, freq, decay):
    f32 = jnp.float32
    t = jnp.linspace(0.0, 1.0, l, dtype=f32)[:, None]
    bands = jnp.linspace(1e-4, HY_BANDS - 1, HY_BANDS, dtype=f32)
    ang = (2.0 * math.pi / l) * jnp.arange(l, dtype=f32)[:, None] * bands
    feat = jnp.concatenate([t, jnp.cos(ang), -jnp.sin(ang)], axis=-1)
    hid = jnp.sin(freq[0] * (feat @ w1 + b1))
    hid = jnp.sin(freq[1] * (hid @ w2 + b2))
    h_fwd, h_bwd = jnp.split((hid @ w3).astype(f32), 2, axis=-1)
    h_fwd = h_fwd * jnp.exp(-t * jnp.abs(decay[0]))
    h_bwd = h_bwd * jnp.exp(-t * jnp.abs(decay[1]))
    kern = jnp.concatenate([h_fwd, jnp.zeros_like(h_fwd[:1]), jnp.flip(h_bwd[1:], axis=0)], axis=0)
    return kern / (jnp.sum(jnp.abs(kern), axis=0, keepdims=True) + EPS)


def hyena_mixer(hn, n_rows, w_in, w_conv, w1, b1, w2, b2, w3, freq, decay, bias, w_out):
    b, l, d = hn.shape
    u = row_conv(hn @ w_in, w_conv, n_rows)
    x0, x1, v = jnp.split(u, 3, axis=-1)
    v = (v * x1).astype(jnp.float32)
    kern = hyena_filter(l, w1, b1, w2, b2, w3, freq, decay)
    y = jnp.fft.irfft(jnp.fft.rfft(v, n=2 * l, axis=1) * jnp.fft.rfft(kern, axis=0),
                      n=2 * l, axis=1)[:, :l]
    y = y + v * bias.astype(jnp.float32)
    return (y * x0.astype(jnp.float32)).astype(hn.dtype) @ w_out


def expert_choice_moe(hn, w_router, w_gate, w_up, w_down):
    b, l, d = hn.shape
    n = b * l
    cap = EC_FACTOR * n // N_EXPERTS
    xt = hn.reshape(n, d)
    aff = jax.nn.softmax((xt @ w_router).astype(jnp.float32), axis=-1)
    gate, idx = lax.top_k(aff.T, cap)
    xs = xt[idx]
    hid = jax.nn.silu(jnp.einsum('ecd,edf->ecf', xs, w_gate)) * jnp.einsum('ecd,edf->ecf', xs, w_up)
    ye = jnp.einsum('ecf,efd->ecd', hid, w_down) * gate[..., None].astype(hn.dtype)
    out = jnp.zeros_like(xt).at[idx.reshape(-1)].add(ye.reshape(-1, d))
    return out.reshape(b, l, d)


def trunk(x, cvec, n_rows, s_init, ada_w, ada_b, norm1_g, norm2_g,
          dn_w_in, dn_conv, dn_a_log, dn_dt_bias, dn_norm_g, dn_w_out,
          hy_w_in, hy_conv, hy_w1, hy_b1, hy_w2, hy_b2, hy_w3, hy_freq, hy_decay, hy_bias, hy_w_out,
          moe_router, moe_w_gate, moe_w_up, moe_w_down, final_g):
    states = []
    for i in range(DEPTH):
        sh1, sc1, g1, sh2, sc2, g2 = ada_mod(cvec, ada_w[i], ada_b[i])
        hn = rms_norm(x, norm1_g[i]) * (1 + sc1) + sh1
        j = i // N_MIXERS
        if i % N_MIXERS == 0:
            y, s = deltanet_mixer(hn, n_rows, s_init[:, j], dn_w_in[j], dn_conv[j], dn_a_log[j],
                                  dn_dt_bias[j], dn_norm_g[j], dn_w_out[j])
            states.append(s)
        else:
            y = hyena_mixer(hn, n_rows, hy_w_in[j], hy_conv[j], hy_w1[j], hy_b1[j], hy_w2[j], hy_b2[j],
                            hy_w3[j], hy_freq[j], hy_decay[j], hy_bias[j], hy_w_out[j])
        x = x + g1 * y
        hn = rms_norm(x, norm2_g[i]) * (1 + sc2) + sh2
        x = x + g2 * expert_choice_moe(hn, moe_router[i], moe_w_gate[i], moe_w_up[i], moe_w_down[i])
    return rms_norm(x, final_g), jnp.stack(states, axis=1)


def setup_inputs(seed: int = 0) -> dict:
    key = jax.random.key(seed)
    ks = iter(jax.random.split(key, 40))
    f32 = jnp.float32
    d, h, dk, e, ff = D_MODEL, DN_HEADS, DN_HEAD_DIM, N_EXPERTS, EXPERT_FF

    def nrm(shape, scale):
        return jax.random.normal(next(ks), shape, f32) * scale

    def gain(shape):
        return 1.0 + nrm(shape, 0.02)

    x_prompt = nrm((BATCH, SEQ, d), 1.0)
    x_sample = nrm((DEC_BATCH, DEC_SEQ, d), 1.0)
    state_dn = nrm((DEC_BATCH, N_DN, 2, h, dk, dk), 0.1)
    c = nrm((DEC_BATCH, d), 1.0)
    c_ctx = nrm((d,), 1.0)
    dt = jnp.exp(jax.random.uniform(next(ks), (N_DN, 2, h), f32, math.log(1e-3), math.log(1e-1)))
    a_log = jnp.log(jax.random.uniform(next(ks), (N_DN, 2, h), f32, 1.0, 16.0))
    decay_base = jnp.linspace(HY_DECAY_SLOW, HY_DECAY_FAST, HY_WIDTH, dtype=f32)
    return {
        "x_prompt": x_prompt,
        "x_sample": x_sample,
        "state_dn": state_dn,
        "c": c,
        "c_ctx": c_ctx,
        "ada_w": nrm((DEPTH, d, N_MOD * d), d ** -0.5),
        "ada_b": nrm((DEPTH, N_MOD * d), 0.02),
        "norm1_g": gain((DEPTH, d)),
        "norm2_g": gain((DEPTH, d)),
        "dn_w_in": nrm((N_DN, d, 4 * d + 4 * h), d ** -0.5),
        "dn_conv": nrm((N_DN, 3 * d, DN_CONV), DN_CONV ** -0.5),
        "dn_a_log": a_log,
        "dn_dt_bias": dt + jnp.log(-jnp.expm1(-dt)),
        "dn_norm_g": gain((N_DN, dk)),
        "dn_w_out": nrm((N_DN, d, d), d ** -0.5),
        "hy_w_in": nrm((N_HY, d, 3 * HY_WIDTH), d ** -0.5),
        "hy_conv": nrm((N_HY, 3 * HY_WIDTH, HY_SHORT), HY_SHORT ** -0.5),
        "hy_w1": nrm((N_HY, HY_POS_DIM, HY_HIDDEN), HY_POS_DIM ** -0.5),
        "hy_b1": nrm((N_HY, HY_HIDDEN), 0.02),
        "hy_w2": nrm((N_HY, HY_HIDDEN, HY_HIDDEN), HY_HIDDEN ** -0.5),
        "hy_b2": nrm((N_HY, HY_HIDDEN), 0.02),
        "hy_w3": nrm((N_HY, HY_HIDDEN, 2 * HY_WIDTH), HY_HIDDEN ** -0.5),
        "hy_freq": gain((N_HY, 2, HY_HIDDEN)),
        "hy_decay": decay_base * (1.0 + nrm((N_HY, 2, HY_WIDTH), 0.05)),
        "hy_bias": nrm((N_HY, HY_WIDTH), 0.5),
        "hy_w_out": nrm((N_HY, HY_WIDTH, d), HY_WIDTH ** -0.5),
        "moe_router": nrm((DEPTH, d, e), d ** -0.5),
        "moe_w_gate": nrm((DEPTH, e, d, ff), d ** -0.5),
        "moe_w_up": nrm((DEPTH, e, d, ff), d ** -0.5),
        "moe_w_down": nrm((DEPTH, e, ff, d), ff ** -0.5),
        "final_g": gain((d,)),
    }


def reference(x_prompt, x_sample, state_dn, c, c_ctx, ada_w, ada_b, norm1_g, norm2_g,
              dn_w_in, dn_conv, dn_a_log, dn_dt_bias, dn_norm_g, dn_w_out,
              hy_w_in, hy_conv, hy_w1, hy_b1, hy_w2, hy_b2, hy_w3, hy_freq, hy_decay, hy_bias, hy_w_out,
              moe_router, moe_w_gate, moe_w_up, moe_w_down, final_g):
    weights = (ada_w, ada_b, norm1_g, norm2_g,
               dn_w_in, dn_conv, dn_a_log, dn_dt_bias, dn_norm_g, dn_w_out,
               hy_w_in, hy_conv, hy_w1, hy_b1, hy_w2, hy_b2, hy_w3, hy_freq, hy_decay, hy_bias, hy_w_out,
               moe_router, moe_w_gate, moe_w_up, moe_w_down, final_g)
    ctx_init = jnp.zeros((x_prompt.shape[0], N_DN, 2, DN_HEADS, DN_HEAD_DIM, DN_HEAD_DIM), jnp.float32)
    y_prompt, new_state_dn = trunk(x_prompt, c_ctx[None, :], 1, ctx_init, *weights)
    rows = x_sample.shape[1] // GRID_W
    y_sample, _ = trunk(x_sample, c, rows, state_dn, *weights)
    return (y_prompt, y_sample, new_state_dn)
```

```python
import functools
import math

import jax
import jax.numpy as jnp
import numpy as np
from jax import lax
from jax.experimental import pallas as pl
from jax.experimental.pallas import tpu as pltpu

EPS = 1e-6
N_MOD = 6
GRID_W = 64
DN_CHUNK = 64
EC_FACTOR = 2
HY_BANDS = 16
V7X_VMEM_LIMIT = 56 * 1024 * 1024

F32 = jnp.float32
BF16 = jnp.bfloat16
HI = lax.Precision.HIGHEST


def _cparams(sem):
    return pltpu.CompilerParams(dimension_semantics=sem, vmem_limit_bytes=V7X_VMEM_LIMIT)


def _group_of_row(r0, n_ctx, dec_seq):
    return jnp.where(r0 < n_ctx, 0, 1 + (r0 - n_ctx) // dec_seq)


def _pick(n, cands):
    for c in cands:
        if n % c == 0:
            return c
    return n


def _norm_mod_kernel(x_ref, g_ref, sc_ref, sh_ref, o_ref):
    x = x_ref[...]
    y = x * lax.rsqrt(jnp.mean(x * x, axis=-1, keepdims=True) + EPS)
    y = y * g_ref[...]
    o_ref[...] = (y * (1.0 + sc_ref[0]) + sh_ref[0]).astype(o_ref.dtype)


def _norm_mod_router_kernel(x_ref, g_ref, sc_ref, sh_ref, wr_ref, o_ref, aff_ref):
    x = x_ref[...]
    y = x * lax.rsqrt(jnp.mean(x * x, axis=-1, keepdims=True) + EPS)
    y = y * g_ref[...]
    hn = y * (1.0 + sc_ref[0]) + sh_ref[0]
    o_ref[...] = hn.astype(o_ref.dtype)
    logits = jnp.dot(hn, wr_ref[...], precision=HI, preferred_element_type=F32)
    m = jnp.max(logits, axis=-1, keepdims=True)
    e = jnp.exp(logits - m)
    aff_ref[...] = e / jnp.sum(e, axis=-1, keepdims=True)


def norm_mod(x, g, sc, sh, n_ctx, dec_seq, out_dtype, w_router=None):
    m, d = x.shape
    tm = _pick(m, (256, 128, 64, 32, 16, 8))
    assert n_ctx % tm == 0 and dec_seq % tm == 0
    grp = lambda i: (_group_of_row(i * tm, n_ctx, dec_seq), 0, 0)
    in_specs = [
        pl.BlockSpec((tm, d), lambda i: (i, 0)),
        pl.BlockSpec((1, d), lambda i: (0, 0)),
        pl.BlockSpec((1, 1, d), grp),
        pl.BlockSpec((1, 1, d), grp),
    ]
    args = [x, g.reshape(1, d), sc[:, None, :], sh[:, None, :]]
    if w_router is None:
        return pl.pallas_call(
            _norm_mod_kernel,
            grid=(m // tm,),
            in_specs=in_specs,
            out_specs=pl.BlockSpec((tm, d), lambda i: (i, 0)),
            out_shape=jax.ShapeDtypeStruct((m, d), out_dtype),
            compiler_params=_cparams(("parallel",)),
            name="norm_mod",
        )(*args)
    e = w_router.shape[1]
    return pl.pallas_call(
        _norm_mod_router_kernel,
        grid=(m // tm,),
        in_specs=in_specs + [pl.BlockSpec((d, e), lambda i: (0, 0))],
        out_specs=[pl.BlockSpec((tm, d), lambda i: (i, 0)),
                   pl.BlockSpec((tm, e), lambda i: (i, 0))],
        out_shape=[jax.ShapeDtypeStruct((m, d), out_dtype),
                   jax.ShapeDtypeStruct((m, e), F32)],
        compiler_params=_cparams(("parallel",)),
        name="norm_mod_router",
    )(*args, w_router)


def _final_norm_kernel(x_ref, g_ref, o_ref):
    x = x_ref[...]
    y = x * lax.rsqrt(jnp.mean(x * x, axis=-1, keepdims=True) + EPS)
    o_ref[...] = y * g_ref[...]


def final_norm(x, g):
    m, d = x.shape
    tm = _pick(m, (256, 128, 64, 32, 16, 8))
    return pl.pallas_call(
        _final_norm_kernel,
        grid=(m // tm,),
        in_specs=[pl.BlockSpec((tm, d), lambda i: (i, 0)),
                  pl.BlockSpec((1, d), lambda i: (0, 0))],
        out_specs=pl.BlockSpec((tm, d), lambda i: (i, 0)),
        out_shape=jax.ShapeDtypeStruct((m, d), F32),
        compiler_params=_cparams(("parallel",)),
        name="final_norm",
    )(x, g.reshape(1, d))


def _mm_kernel(a_ref, w_ref, o_ref):
    acc = jnp.dot(a_ref[...], w_ref[...].astype(BF16), preferred_element_type=F32)
    o_ref[...] = acc.astype(o_ref.dtype)


def _mm_bias_kernel(a_ref, w_ref, b_ref, o_ref):
    acc = jnp.dot(a_ref[...], w_ref[...].astype(BF16), preferred_element_type=F32)
    o_ref[...] = (acc + b_ref[...]).astype(o_ref.dtype)


def _mm_resid_kernel(a_ref, w_ref, x_ref, gate_ref, o_ref):
    acc = jnp.dot(a_ref[...], w_ref[...].astype(BF16), preferred_element_type=F32)
    o_ref[...] = x_ref[...] + gate_ref[0] * acc


def matmul(a, w, *, widx=None, col0=0, ncols=None, out_dtype=F32, bias=None, resid=None, tm=None, tn=None):
    m, k = a.shape
    n = w.shape[-1] - col0 if ncols is None else ncols
    tm = tm or _pick(m, (1024, 512, 256, 128, 64, 32, 16, 8))
    tn = tn or _pick(n, (512, 256, 128))
    assert col0 % tn == 0 and m % tm == 0 and n % tn == 0
    cb = col0 // tn
    grid = (m // tm, n // tn)
    a_spec = pl.BlockSpec((tm, k), lambda i, j: (i, 0))
    if widx is None:
        w_spec = pl.BlockSpec((k, tn), lambda i, j: (0, cb + j))
    else:
        w_spec = pl.BlockSpec((None, k, tn), lambda i, j: (widx, 0, cb + j))
    o_spec = pl.BlockSpec((tm, tn), lambda i, j: (i, j))
    out_shape = jax.ShapeDtypeStruct((m, n), out_dtype)
    cp = _cparams(("parallel", "parallel"))
    if resid is not None:
        x, gate, n_ctx, dec_seq = resid
        assert n_ctx % tm == 0 and dec_seq % tm == 0
        g_spec = pl.BlockSpec((1, 1, tn), lambda i, j: (_group_of_row(i * tm, n_ctx, dec_seq), 0, j))
        return pl.pallas_call(
            _mm_resid_kernel, grid=grid,
            in_specs=[a_spec, w_spec, o_spec, g_spec], out_specs=o_spec,
            out_shape=out_shape, compiler_params=cp, name="matmul_resid",
        )(a, w, x, gate[:, None, :])
    if bias is not None:
        b_spec = pl.BlockSpec((1, tn), lambda i, j: (0, j))
        return pl.pallas_call(
            _mm_bias_kernel, grid=grid,
            in_specs=[a_spec, w_spec, b_spec], out_specs=o_spec,
            out_shape=out_shape, compiler_params=cp, name="matmul_bias",
        )(a, w, bias.reshape(1, n))
    return pl.pallas_call(
        _mm_kernel, grid=grid,
        in_specs=[a_spec, w_spec], out_specs=o_spec,
        out_shape=out_shape, compiler_params=cp, name="matmul",
    )(a, w)


def _silu(x):
    return x * (1.0 / (1.0 + jnp.exp(-x)))


def _softplus(x):
    return jnp.maximum(x, 0.0) + jnp.log1p(jnp.exp(-jnp.abs(x)))


def _split_bf16(x):
    hi = x.astype(BF16)
    lo = (x - hi.astype(F32)).astype(BF16)
    return hi, lo


def _seg_conv(x, w_ref, seg):
    r, _ = x.shape
    k = w_ref.shape[0]
    p = k // 2
    assert seg & (seg - 1) == 0
    pos = jnp.bitwise_and(lax.broadcasted_iota(jnp.int32, x.shape, 0), seg - 1)
    y = x * w_ref[p:p + 1, :]
    for j in range(k):
        off = j - p
        if off == 0:
            continue
        shifted = pltpu.roll(x, (-off) % r, axis=0)
        ok = jnp.logical_and(pos + off >= 0, pos + off < seg)
        y = y + jnp.where(ok, shifted, 0.0) * w_ref[j:j + 1, :]
    return y


def _dn_kernel(q_ref, k_ref, v_ref, z_ref, gt_ref, wq_ref, wk_ref, wv_ref, arow_ref, dtrow_ref,
               ng_ref, s0_ref, o_ref, sfin_ref,
               qn_ref, kn_ref, vn_ref, of_ref, ob_ref, sf_ref, sb_ref, vnew_ref, *, seg, chunk, nheads):
    l, dk = q_ref.shape
    c = chunk
    n = l // c
    h = pl.program_id(1)

    def prep(i, carry):
        rows = pl.ds(pl.multiple_of(i * c, c), c)
        q = _silu(_seg_conv(q_ref[rows, :], wq_ref, seg))
        k = _silu(_seg_conv(k_ref[rows, :], wk_ref, seg))
        v = _silu(_seg_conv(v_ref[rows, :], wv_ref, seg))
        q = q * lax.rsqrt(jnp.sum(q * q, axis=-1, keepdims=True) + EPS) * (dk ** -0.5)
        k = k * lax.rsqrt(jnp.sum(k * k, axis=-1, keepdims=True) + EPS)
        qn_ref[rows, :] = q
        kn_ref[rows, :] = k
        vn_ref[rows, :] = v
        return carry

    lax.fori_loop(0, n, prep, 0)

    sf_ref[...] = s0_ref[0, 0, 0]
    sb_ref[...] = s0_ref[0, 1, 0]

    cc = DN_CHUNK
    ri = lax.broadcasted_iota(jnp.int32, (c, c), 0)
    ci = lax.broadcasted_iota(jnp.int32, (c, c), 1)
    sh = int(math.log2(cc))
    same = jnp.right_shift(ri, sh) == jnp.right_shift(ci, sh)
    lane = lax.broadcasted_iota(jnp.int32, (c, gt_ref.shape[1]), 1)
    nt = (((1,), (1,)), ((), ()))
    tn = (((0,), (0,)), ((), ()))
    vnew_ref[...] = jnp.zeros_like(vnew_ref)

    def col(tile, idx):
        return jnp.sum(jnp.where(lane == idx, tile, 0.0), axis=-1, keepdims=True)

    same_lv = {lv: jnp.right_shift(ri, lv) == jnp.right_shift(ci, lv) for lv in range(3, sh + 1)}
    same8 = same_lv[3]
    eye = (ri == ci).astype(F32)

    def mm13(a_b, y):
        y_hi, y_lo = _split_bf16(y)
        return jnp.dot(a_b, y_hi, preferred_element_type=F32) + jnp.dot(a_b, y_lo, preferred_element_type=F32)

    def mm31(y, a_b):
        y_hi, y_lo = _split_bf16(y)
        return jnp.dot(y_hi, a_b, preferred_element_type=F32) + jnp.dot(y_lo, a_b, preferred_element_type=F32)

    def mm33(y, w):
        y_hi, y_lo = _split_bf16(y)
        w_hi, w_lo = _split_bf16(w)
        return (jnp.dot(y_hi, w_hi, preferred_element_type=F32) + jnp.dot(y_hi, w_lo, preferred_element_type=F32)
                + jnp.dot(y_lo, w_hi, preferred_element_type=F32))

    def super_dir(i, d, s_ref, out_ref):
        rows = pl.ds(pl.multiple_of(i * c, c), c)
        gt = gt_ref[rows, :]
        beta = col(1.0 / (1.0 + jnp.exp(-gt)), d * nheads + h)
        g_all = -jnp.exp(arow_ref[...]) * _softplus(gt + dtrow_ref[...])
        if d == 0:
            incl = jnp.logical_and(same, ri >= ci)
            strict = jnp.logical_and(same, ri > ci)
        else:
            incl = jnp.logical_and(same, ri <= ci)
            strict = jnp.logical_and(same, ri < ci)
        gidx = 2 * nheads + d * nheads + h
        gc = col(jnp.dot(incl.astype(F32), g_all, precision=HI, preferred_element_type=F32), gidx)
        tot = col(jnp.dot(same.astype(F32), g_all, precision=HI, preferred_element_type=F32), gidx)
        gcb = jnp.broadcast_to(gc, (c, c))
        diff = jnp.where(incl, gcb - jnp.transpose(gcb), 0.0)
        decay = jnp.where(incl, jnp.exp(diff), 0.0)
        q = qn_ref[rows, :]
        k = kn_ref[rows, :]
        v = vn_ref[rows, :]
        kb = k * beta
        kbf = k.astype(BF16)
        a = lax.dot_general(kb.astype(BF16), kbf, nt, preferred_element_type=F32)
        a = jnp.where(strict, a * decay, 0.0)
        attn = (lax.dot_general(q.astype(BF16), kbf, nt, preferred_element_type=F32) * decay).astype(BF16)
        ab = a.astype(BF16)
        zero = jnp.zeros_like(ab)
        n8 = jnp.where(same8, ab, zero)
        n2 = jnp.dot(n8, n8, preferred_element_type=F32)
        n4 = mm33(n2, n2)
        inv = mm13((eye - n8.astype(F32)).astype(BF16), eye + n2)
        inv = mm33(inv, eye + n4)
        for lv in range(3, sh):
            off = jnp.where(jnp.logical_and(same_lv[lv + 1], jnp.logical_not(same_lv[lv])), ab, zero)
            inv = inv - mm33(mm31(inv, off), inv)
        x = mm33(inv, jnp.concatenate([v * beta, kb * jnp.exp(gc)], axis=-1))
        qg = (q * jnp.exp(gc)).astype(BF16)
        kd = (k * jnp.exp(tot - gc)).astype(BF16)
        etot = jnp.exp(tot)
        nsub = c // cc
        order = range(nsub) if d == 0 else range(nsub - 1, -1, -1)
        for t in order:
            r0 = t * cc
            s = s_ref[...]
            sb = s.astype(BF16)
            u_c = x[r0:r0 + cc, :dk]
            w_c = x[r0:r0 + cc, dk:]
            v_new = u_c - jnp.dot(w_c.astype(BF16), sb, preferred_element_type=F32)
            vnb = v_new.astype(BF16)
            vnew_ref[r0:r0 + cc, :] = vnb
            o = (jnp.dot(qg[r0:r0 + cc, :], sb, preferred_element_type=F32)
                 + jnp.dot(attn[r0:r0 + cc, :], vnew_ref[...], preferred_element_type=F32))
            out_ref[pl.ds(pl.multiple_of(i * c + r0, cc), cc), :] = o
            upd = lax.dot_general(kd[r0:r0 + cc, :], vnb, tn, preferred_element_type=F32)
            s_ref[...] = s * etot[r0:r0 + 1, :] + upd

    def scan(i, carry):
        super_dir(i, 0, sf_ref, of_ref)
        super_dir(n - 1 - i, 1, sb_ref, ob_ref)
        return carry

    lax.fori_loop(0, n, scan, 0)

    sfin_ref[0, 0, 0] = sf_ref[...]
    sfin_ref[0, 1, 0] = sb_ref[...]

    def fin(i, carry):
        rows = pl.ds(pl.multiple_of(i * c, c), c)
        o = of_ref[rows, :] + ob_ref[rows, :]
        o = o * lax.rsqrt(jnp.mean(o * o, axis=-1, keepdims=True) + EPS) * ng_ref[...]
        o_ref[rows, :] = (o * _silu(z_ref[rows, :].astype(F32))).astype(o_ref.dtype)
        return carry

    lax.fori_loop(0, n, fin, 0)


def deltanet_core(qkv, z, gates, w_conv_t, a_row, dt_row, norm_g, s0, *, row0, nseq, seqlen, seg, nheads):
    m, d3 = qkv.shape
    d = d3 // 3
    dk = d // nheads
    assert dk == 128 and row0 % seqlen == 0
    chunk = min(seqlen, 256)
    assert seqlen % chunk == 0 and chunk % seg == 0 and chunk % DN_CHUNK == 0
    rb = row0 // seqlen
    kw = w_conv_t.shape[0]
    g4 = gates.shape[1]

    def colspec(part):
        return pl.BlockSpec((seqlen, dk), lambda b, h: (rb + b, part * nheads + h))

    def wspec(part):
        return pl.BlockSpec((kw, dk), lambda b, h: (0, part * nheads + h))

    kern = functools.partial(_dn_kernel, seg=seg, chunk=chunk, nheads=nheads)
    st_spec = pl.BlockSpec((1, 2, 1, dk, dk), lambda b, h: (b, 0, h, 0, 0))
    out, sfin = pl.pallas_call(
        kern,
        grid=(nseq, nheads),
        in_specs=[colspec(0), colspec(1), colspec(2),
                  pl.BlockSpec((seqlen, dk), lambda b, h: (rb + b, h)),
                  pl.BlockSpec((seqlen, g4), lambda b, h: (rb + b, 0)),
                  wspec(0), wspec(1), wspec(2),
                  pl.BlockSpec((1, g4), lambda b, h: (0, 0)),
                  pl.BlockSpec((1, g4), lambda b, h: (0, 0)),
                  pl.BlockSpec((1, dk), lambda b, h: (0, 0)),
                  st_spec],
        out_specs=[pl.BlockSpec((seqlen, dk), lambda b, h: (b, h)), st_spec],
        out_shape=[jax.ShapeDtypeStruct((nseq * seqlen, d), BF16),
                   jax.ShapeDtypeStruct((nseq, 2, nheads, dk, dk), F32)],
        scratch_shapes=([pltpu.VMEM((seqlen, dk), F32) for _ in range(5)]
                        + [pltpu.VMEM((dk, dk), F32) for _ in range(2)]
                        + [pltpu.VMEM((chunk, dk), BF16)]),
        compiler_params=_cparams(("parallel", "parallel")),
        name="deltanet_core",
    )(qkv, qkv, qkv, z, gates, w_conv_t, w_conv_t, w_conv_t, a_row, dt_row, norm_g.reshape(1, dk), s0)
    return out, sfin


def _dot3(a_hi, a_lo, x):
    x_hi, x_lo = _split_bf16(x)
    acc = jnp.dot(a_hi, x_hi, preferred_element_type=F32)
    acc = acc + jnp.dot(a_hi, x_lo, preferred_element_type=F32)
    return acc + jnp.dot(a_lo, x_hi, preferred_element_type=F32)


def _dft_mats(p):
    n = 2 * p
    k = np.arange(p, dtype=np.float64)[:, None]
    s = np.arange(n, dtype=np.float64)[None, :]
    th = np.pi * (2.0 * k + 1.0) * s / n
    fwd = np.concatenate([np.cos(th), -np.sin(th)], axis=0)
    tht = th[:, :p].T
    inv = np.concatenate([np.cos(tht), -np.sin(tht)], axis=1) * (2.0 / n)
    return fwd, inv


def _hi_lo_const(a):
    a32 = np.asarray(a, np.float32)
    hi = a32.astype(BF16)
    lo = (a32 - hi.astype(np.float32)).astype(BF16)
    return jnp.asarray(hi), jnp.asarray(lo)


def _spec_kernel(ah_ref, al_ref, t_ref, o_ref):
    o_ref[0] = _dot3(ah_ref[...], al_ref[...], t_ref[0])


def filter_spectra(taps, p):
    nd, n, c = taps.shape
    fwd, _ = _dft_mats(p)
    fh, fl = _hi_lo_const(fwd)
    tn = _pick(c, (512, 256, 128))
    return pl.pallas_call(
        _spec_kernel,
        grid=(nd, c // tn),
        in_specs=[pl.BlockSpec((n, n), lambda d, j: (0, 0)),
                  pl.BlockSpec((n, n), lambda d, j: (0, 0)),
                  pl.BlockSpec((1, n, tn), lambda d, j: (d, 0, j))],
        out_specs=pl.BlockSpec((1, n, tn), lambda d, j: (d, 0, j)),
        out_shape=jax.ShapeDtypeStruct((nd, n, c), F32),
        compiler_params=_cparams(("parallel", "parallel")),
        name="hyena_filter_spectra",
    )(fh, fl, taps)


def _hy_kernel(x0_ref, x1_ref, v_ref, w0_ref, w1_ref, w2_ref, bias_ref, spec_ref,
               fh_ref, fl_ref, ih_ref, il_ref, o_ref, vv_ref, vh_ref, acc_ref, *, seg, p):
    l, tc = v_ref.shape
    nb = l // p

    def fwd(j, carry):
        rows = pl.ds(pl.multiple_of(j * p, p), p)
        vv = _seg_conv(v_ref[rows, :], w2_ref, seg) * _seg_conv(x1_ref[rows, :], w1_ref, seg)
        vv_ref[rows, :] = vv
        vh_ref[j] = _dot3(fh_ref[...], fl_ref[...], vv)
        return carry

    lax.fori_loop(0, nb, fwd, 0)

    rt = 64

    def out_block(i, carry):
        def row_tile(r, c2):
            rr = pl.ds(pl.multiple_of(r * rt, rt), rt)
            ri = pl.ds(pl.multiple_of(p + r * rt, rt), rt)

            def mac(j, acc):
                are, aim = acc
                d = i - j + (nb - 1)
                fre = spec_ref[d, rr, :]
                fim = spec_ref[d, ri, :]
                xre = vh_ref[j, rr, :]
                xim = vh_ref[j, ri, :]
                return (are + fre * xre - fim * xim, aim + fre * xim + fim * xre)

            zero = jnp.zeros((rt, tc), F32)
            are, aim = lax.fori_loop(0, nb, mac, (zero, zero))
            acc_ref[rr, :] = are
            acc_ref[ri, :] = aim
            return c2

        lax.fori_loop(0, p // rt, row_tile, 0)
        y = _dot3(ih_ref[...], il_ref[...], acc_ref[...])
        rows = pl.ds(pl.multiple_of(i * p, p), p)
        y = y + vv_ref[rows, :] * bias_ref[...]
        o_ref[rows, :] = (y * _seg_conv(x0_ref[rows, :], w0_ref, seg)).astype(o_ref.dtype)
        return carry

    lax.fori_loop(0, nb, out_block, 0)


def hyena_core(u, w_conv_t, bias, spec, *, row0, nseq, seqlen, seg, p):
    m, c3 = u.shape
    c = c3 // 3
    tc = 128
    assert c % tc == 0 and row0 % seqlen == 0 and seqlen % p == 0 and p % seg == 0
    nct = c // tc
    rb = row0 // seqlen
    nd = spec.shape[0]
    kw = w_conv_t.shape[0]
    fwd, inv = _dft_mats(p)
    fh, fl = _hi_lo_const(fwd[:, :p])
    ih, il = _hi_lo_const(inv)

    def colspec(part):
        return pl.BlockSpec((seqlen, tc), lambda j, b: (rb + b, part * nct + j))

    def wspec(part):
        return pl.BlockSpec((kw, tc), lambda j, b: (0, part * nct + j))

    const = lambda shape: pl.BlockSpec(shape, lambda j, b: (0,) * len(shape))
    kern = functools.partial(_hy_kernel, seg=seg, p=p)
    return pl.pallas_call(
        kern,
        grid=(nct, nseq),
        in_specs=[colspec(0), colspec(1), colspec(2), wspec(0), wspec(1), wspec(2),
                  pl.BlockSpec((1, tc), lambda j, b: (0, j)),
                  pl.BlockSpec((nd, 2 * p, tc), lambda j, b: (0, 0, j)),
                  const((2 * p, p)), const((2 * p, p)), const((p, 2 * p)), const((p, 2 * p))],
        out_specs=pl.BlockSpec((seqlen, tc), lambda j, b: (b, j)),
        out_shape=jax.ShapeDtypeStruct((nseq * seqlen, c), BF16),
        scratch_shapes=[pltpu.VMEM((seqlen, tc), F32),
                        pltpu.VMEM((seqlen // p, 2 * p, tc), F32),
                        pltpu.VMEM((2 * p, tc), F32)],
        compiler_params=_cparams(("parallel", "parallel")),
        name="hyena_core",
    )(u, u, u, w_conv_t, w_conv_t, w_conv_t, bias.reshape(1, c), spec, fh, fl, ih, il)


def hyena_filter_taps(l, p, w1, b1, w2, b2, w3, freq, decay):
    t = jnp.linspace(0.0, 1.0, l, dtype=F32)[:, None]
    bands = jnp.linspace(1e-4, HY_BANDS - 1, HY_BANDS, dtype=F32)
    ang = (2.0 * math.pi / l) * jnp.arange(l, dtype=F32)[:, None] * bands
    feat = jnp.concatenate([t, jnp.cos(ang), -jnp.sin(ang)], axis=-1)
    hid = jnp.sin(freq[0] * (jnp.dot(feat, w1, precision=HI) + b1))
    hid = jnp.sin(freq[1] * (jnp.dot(hid, w2, precision=HI) + b2))
    h_fwd, h_bwd = jnp.split(jnp.dot(hid, w3, precision=HI).astype(F32), 2, axis=-1)
    h_fwd = h_fwd * jnp.exp(-t * jnp.abs(decay[0]))
    h_bwd = h_bwd * jnp.exp(-t * jnp.abs(decay[1]))
    kern = jnp.concatenate([h_fwd, jnp.zeros_like(h_fwd[:1]), jnp.flip(h_bwd[1:], axis=0)], axis=0)
    kern = kern / (jnp.sum(jnp.abs(kern), axis=0, keepdims=True) + EPS)
    nb = l // p
    n = 2 * p
    mi = np.arange(n)
    delta = np.where(mi < p, mi, mi - n)
    sign = np.where(mi < p, 1.0, -1.0) * (mi != p)
    off = np.arange(-(nb - 1), nb)[:, None] * p + delta[None, :]
    valid = np.abs(off) < l
    idx = np.where(valid, off % (2 * l), 0).astype(np.int32)
    coef = (valid * sign[None, :]).astype(np.float32)
    return kern[jnp.asarray(idx)] * jnp.asarray(coef)[:, :, None]


def _gather_kernel(idx_ref, src_ref, o_ref, stage_ref, sem):
    i = pl.program_id(0)
    tg = stage_ref.shape[0]

    def issue(r, carry):
        row = idx_ref[i * tg + r]
        pltpu.make_async_copy(src_ref.at[pl.ds(row, 1)], stage_ref.at[pl.ds(r, 1)], sem).start()
        return carry

    lax.fori_loop(0, tg, issue, 0)

    def drain(r, carry):
        pltpu.make_async_copy(src_ref.at[pl.ds(0, 1)], stage_ref.at[pl.ds(r, 1)], sem).wait()
        return carry

    lax.fori_loop(0, tg, drain, 0)
    o_ref[...] = stage_ref[...].astype(o_ref.dtype)


def gather_rows(src, idx, tg=256):
    npick = idx.shape[0]
    d = src.shape[1]
    assert npick % tg == 0
    return pl.pallas_call(
        _gather_kernel,
        grid_spec=pltpu.PrefetchScalarGridSpec(
            num_scalar_prefetch=1,
            grid=(npick // tg,),
            in_specs=[pl.BlockSpec(memory_space=pl.ANY)],
            out_specs=pl.BlockSpec((tg, d), lambda i, idx_ref: (i, 0)),
            scratch_shapes=[pltpu.VMEM((tg, d), F32), pltpu.SemaphoreType.DMA(())],
        ),
        out_shape=jax.ShapeDtypeStruct((npick, d), BF16),
        compiler_params=_cparams(("arbitrary",)),
        name="moe_gather",
    )(idx, src)


def _ffn_kernel(xs_ref, wg_ref, wu_ref, wd_ref, o_ref, hid_ref, *, nf, tf):
    j = pl.program_id(1)

    @pl.when(j < nf)
    def _():
        xs = xs_ref[...]
        g = jnp.dot(xs, wg_ref[0].astype(BF16), preferred_element_type=F32)
        u = jnp.dot(xs, wu_ref[0].astype(BF16), preferred_element_type=F32)
        hid_ref[j] = (_silu(g) * u).astype(BF16)

    @pl.when(j >= nf)
    def _():
        acc = jnp.dot(hid_ref[0], wd_ref[0, 0:tf, :].astype(BF16), preferred_element_type=F32)
        for f in range(1, nf):
            acc = acc + jnp.dot(hid_ref[f], wd_ref[0, f * tf:(f + 1) * tf, :].astype(BF16),
                                preferred_element_type=F32)
        o_ref[...] = acc


def expert_ffn(xs, w_gate, w_up, w_down, lyr, cap):
    npick, d = xs.shape
    _, e, _, ff = w_gate.shape
    nse = npick // cap
    tf = _pick(ff, (256, 128))
    td = _pick(d, (512, 256, 128))
    nf, ndt = ff // tf, d // td
    kern = functools.partial(_ffn_kernel, nf=nf, tf=tf)
    return pl.pallas_call(
        kern,
        grid=(nse, nf + ndt),
        in_specs=[pl.BlockSpec((cap, d), lambda s, j: (s, 0)),
                  pl.BlockSpec((None, 1, d, tf), lambda s, j: (lyr, s % e, 0, jnp.minimum(j, nf - 1))),
                  pl.BlockSpec((None, 1, d, tf), lambda s, j: (lyr, s % e, 0, jnp.minimum(j, nf - 1))),
                  pl.BlockSpec((None, 1, ff, td), lambda s, j: (lyr, s % e, 0, jnp.maximum(j - nf, 0)))],
        out_specs=pl.BlockSpec((cap, td), lambda s, j: (s, jnp.maximum(j - nf, 0))),
        out_shape=jax.ShapeDtypeStruct((npick, d), F32),
        scratch_shapes=[pltpu.VMEM((nf, cap, tf), BF16)],
        compiler_params=_cparams(("parallel", "arbitrary")),
        name="moe_expert_ffn",
    )(xs, w_gate, w_up, w_down)


def _scatter_kernel(idx_ref, ye_ref, coef_ref, sel_ref, ga_ref, gb_ref, x_in_ref, x_ref,
                    stage_ref, sem_r, sem_w):
    del x_in_ref
    i = pl.program_id(0)
    tg = stage_ref.shape[0]

    def rd(r, carry):
        row = idx_ref[i * tg + r]
        pltpu.make_async_copy(x_ref.at[pl.ds(row, 1)], stage_ref.at[pl.ds(r, 1)], sem_r).start()
        return carry

    lax.fori_loop(0, tg, rd, 0)

    def rd_wait(r, carry):
        pltpu.make_async_copy(x_ref.at[pl.ds(0, 1)], stage_ref.at[pl.ds(r, 1)], sem_r).wait()
        return carry

    lax.fori_loop(0, tg, rd_wait, 0)
    gvec = jnp.where(sel_ref[...] > 0.5, gb_ref[0], ga_ref[0])
    stage_ref[...] = stage_ref[...] + gvec * (coef_ref[...] * ye_ref[...])

    def wr(r, carry):
        row = idx_ref[i * tg + r]
        pltpu.make_async_copy(stage_ref.at[pl.ds(r, 1)], x_ref.at[pl.ds(row, 1)], sem_w).start()
        return carry

    lax.fori_loop(0, tg, wr, 0)

    def wr_wait(r, carry):
        pltpu.make_async_copy(stage_ref.at[pl.ds(r, 1)], x_ref.at[pl.ds(0, 1)], sem_w).wait()
        return carry

    lax.fori_loop(0, tg, wr_wait, 0)


def scatter_add_rows(x, ye, idx, coef, sel, gab, picks_per_stream, tg=256):
    npick, d = ye.shape
    assert npick % tg == 0 and picks_per_stream % tg == 0
    cps = picks_per_stream // tg
    return pl.pallas_call(
        _scatter_kernel,
        grid_spec=pltpu.PrefetchScalarGridSpec(
            num_scalar_prefetch=1,
            grid=(npick // tg,),
            in_specs=[pl.BlockSpec((tg, d), lambda i, idx_ref: (i, 0)),
                      pl.BlockSpec((tg, 1), lambda i, idx_ref: (i, 0)),
                      pl.BlockSpec((tg, 1), lambda i, idx_ref: (i, 0)),
                      pl.BlockSpec((1, 1, d), lambda i, idx_ref: (2 * (i // cps), 0, 0)),
                      pl.BlockSpec((1, 1, d), lambda i, idx_ref: (2 * (i // cps) + 1, 0, 0)),
                      pl.BlockSpec(memory_space=pl.ANY)],
            out_specs=pl.BlockSpec(memory_space=pl.ANY),
            scratch_shapes=[pltpu.VMEM((tg, d), F32), pltpu.SemaphoreType.DMA(()),
                            pltpu.SemaphoreType.DMA(())],
        ),
        out_shape=jax.ShapeDtypeStruct(x.shape, x.dtype),
        input_output_aliases={6: 0},
        compiler_params=_cparams(("arbitrary",)),
        name="moe_scatter_add",
    )(idx, ye, coef, sel, gab, gab, x)


def moe_layer(x, hn, aff, g2, w_gate, w_up, w_down, lyr, n_ctx, dec_seq):
    m, d = x.shape
    e = aff.shape[1]
    n_dec = m - n_ctx
    assert n_ctx == n_dec, "merged MoE kernels need equal routed-set sizes"
    cap = EC_FACTOR * n_ctx // e
    gate0, idx0 = lax.top_k(aff[:n_ctx].T, cap)
    gate1, idx1 = lax.top_k(aff[n_ctx:].T, cap)
    idx = jnp.concatenate([idx0.reshape(-1), idx1.reshape(-1) + n_ctx]).astype(jnp.int32)
    coef = jnp.concatenate([gate0.reshape(-1), gate1.reshape(-1)])[:, None]
    sel = (idx >= n_ctx + dec_seq).astype(F32)[:, None]
    ngrp = g2.shape[0]
    gab = jnp.stack([g2[0], g2[0], g2[1], g2[min(2, ngrp - 1)]])[:, None, :]
    xs = gather_rows(hn, idx)
    ye = expert_ffn(xs, w_gate, w_up, w_down, lyr, cap)
    return scatter_add_rows(x, ye, idx, coef, sel, gab, e * cap)


def kernel(x_prompt, x_sample, state_dn, c, c_ctx, ada_w, ada_b, norm1_g, norm2_g, dn_w_in, dn_conv, dn_a_log, dn_dt_bias, dn_norm_g, dn_w_out, hy_w_in, hy_conv, hy_w1, hy_b1, hy_w2, hy_b2, hy_w3, hy_freq, hy_decay, hy_bias, hy_w_out, moe_router, moe_w_gate, moe_w_up, moe_w_down, final_g):
    bsz, seq, d = x_prompt.shape
    dbsz, dseq, _ = x_sample.shape
    assert dbsz == 2, "scatter group select handles two latent requests"
    depth = ada_w.shape[0]
    nheads = dn_a_log.shape[2]
    n_ctx = bsz * seq
    n_dec = dbsz * dseq
    x = jnp.concatenate([x_prompt.reshape(n_ctx, d), x_sample.reshape(n_dec, d)], axis=0)

    cvec = jnp.concatenate([c_ctx[None, :], c], axis=0)
    ngrp = cvec.shape[0]
    sc_in = jnp.zeros((16, d), F32).at[:ngrp].set(cvec)
    sc_in = (sc_in * (1.0 / (1.0 + jnp.exp(-sc_in)))).astype(BF16)

    p_ctx = min(seq, 512)
    p_dec = min(dseq, 512)
    states = []
    for i in range(depth):
        mod = matmul(sc_in, ada_w, widx=i, bias=ada_b[i], tm=16)[:ngrp]
        sh1, sc1, g1, sh2, sc2, g2 = [mod[:, t * d:(t + 1) * d] for t in range(N_MOD)]
        hn = norm_mod(x, norm1_g[i], sc1, sh1, n_ctx, dseq, BF16)
        j = i // 2
        if i % 2 == 0:
            qkv = matmul(hn, dn_w_in, widx=j, col0=0, ncols=3 * d)
            z = matmul(hn, dn_w_in, widx=j, col0=3 * d, ncols=d, out_dtype=BF16)
            gates = matmul(hn, dn_w_in, widx=j, col0=4 * d, ncols=4 * nheads, tn=4 * nheads)
            wct = dn_conv[j].T
            a_row = jnp.concatenate([jnp.zeros((2 * nheads,), F32), dn_a_log[j].reshape(-1)])[None, :]
            dt_row = jnp.concatenate([jnp.zeros((2 * nheads,), F32), dn_dt_bias[j].reshape(-1)])[None, :]
            dk = d // nheads
            o_ctx, s_ctx = deltanet_core(qkv, z, gates, wct, a_row, dt_row, dn_norm_g[j],
                                         jnp.zeros((bsz, 2, nheads, dk, dk), F32),
                                         row0=0, nseq=bsz, seqlen=seq, seg=seq, nheads=nheads)
            o_dec, _ = deltanet_core(qkv, z, gates, wct, a_row, dt_row, dn_norm_g[j], state_dn[:, j],
                                     row0=n_ctx, nseq=dbsz, seqlen=dseq, seg=GRID_W, nheads=nheads)
            states.append(s_ctx)
            y_in = jnp.concatenate([o_ctx, o_dec], axis=0)
            x = matmul(y_in, dn_w_out, widx=j, resid=(x, g1, n_ctx, dseq))
        else:
            u = matmul(hn, hy_w_in, widx=j)
            wct = hy_conv[j].T
            filt = (hy_w1[j], hy_b1[j], hy_w2[j], hy_b2[j], hy_w3[j], hy_freq[j], hy_decay[j])
            spec_ctx = filter_spectra(hyena_filter_taps(seq, p_ctx, *filt), p_ctx)
            spec_dec = filter_spectra(hyena_filter_taps(dseq, p_dec, *filt), p_dec)
            y_ctx = hyena_core(u, wct, hy_bias[j], spec_ctx, row0=0, nseq=bsz, seqlen=seq, seg=seq, p=p_ctx)
            y_dec = hyena_core(u, wct, hy_bias[j], spec_dec, row0=n_ctx, nseq=dbsz, seqlen=dseq,
                               seg=GRID_W, p=p_dec)
            y_in = jnp.concatenate([y_ctx, y_dec], axis=0)
            x = matmul(y_in, hy_w_out, widx=j, resid=(x, g1, n_ctx, dseq))
        hn2, aff = norm_mod(x, norm2_g[i], sc2, sh2, n_ctx, dseq, F32, w_router=moe_router[i])
        x = moe_layer(x, hn2, aff, g2, moe_w_gate, moe_w_up, moe_w_down, i, n_ctx, dseq)
    y = final_norm(x, final_g)
    y_prompt = y[:n_ctx].reshape(bsz, seq, d)
    y_sample = y[n_ctx:].reshape(dbsz, dseq, d)
    return (y_prompt, y_sample, jnp.stack(states, axis=1))
```

```python
import functools
import math

import jax
import jax.numpy as jnp
import numpy as np
from jax import lax
from jax.experimental import pallas as pl
from jax.experimental.pallas import tpu as pltpu

EPS = 1e-6
N_MOD = 6
GRID_W = 64
DN_CHUNK = 64
EC_FACTOR = 2
HY_BANDS = 16
V7X_VMEM_LIMIT = 56 * 1024 * 1024

F32 = jnp.float32
BF16 = jnp.bfloat16
HI = lax.Precision.HIGHEST


def _cparams(sem):
    return pltpu.CompilerParams(dimension_semantics=sem, vmem_limit_bytes=V7X_VMEM_LIMIT)


def _group_of_row(r0, n_ctx, dec_seq):
    return jnp.where(r0 < n_ctx, 0, 1 + (r0 - n_ctx) // dec_seq)


def _pick(n, cands):
    for c in cands:
        if n % c == 0:
            return c
    return n


def _norm_mod_kernel(x_ref, g_ref, sc_ref, sh_ref, o_ref):
    x = x_ref[...]
    y = x * lax.rsqrt(jnp.mean(x * x, axis=-1, keepdims=True) + EPS)
    y = y * g_ref[...]
    o_ref[...] = (y * (1.0 + sc_ref[0]) + sh_ref[0]).astype(o_ref.dtype)


def _norm_mod_router_kernel(x_ref, g_ref, sc_ref, sh_ref, wr_ref, o_ref, aff_ref):
    x = x_ref[...]
    y = x * lax.rsqrt(jnp.mean(x * x, axis=-1, keepdims=True) + EPS)
    y = y * g_ref[...]
    hn = y * (1.0 + sc_ref[0]) + sh_ref[0]
    o_ref[...] = hn.astype(o_ref.dtype)
    logits = jnp.dot(hn, wr_ref[...], precision=HI, preferred_element_type=F32)
    m = jnp.max(logits, axis=-1, keepdims=True)
    e = jnp.exp(logits - m)
    aff_ref[...] = e / jnp.sum(e, axis=-1, keepdims=True)


def norm_mod(x, g, sc, sh, n_ctx, dec_seq, out_dtype, w_router=None):
    m, d = x.shape
    tm = _pick(m, (256, 128, 64, 32, 16, 8))
    assert n_ctx % tm == 0 and dec_seq % tm == 0
    grp = lambda i: (_group_of_row(i * tm, n_ctx, dec_seq), 0, 0)
    in_specs = [
        pl.BlockSpec((tm, d), lambda i: (i, 0)),
        pl.BlockSpec((1, d), lambda i: (0, 0)),
        pl.BlockSpec((1, 1, d), grp),
        pl.BlockSpec((1, 1, d), grp),
    ]
    args = [x, g.reshape(1, d), sc[:, None, :], sh[:, None, :]]
    if w_router is None:
        return pl.pallas_call(
            _norm_mod_kernel,
            grid=(m // tm,),
            in_specs=in_specs,
            out_specs=pl.BlockSpec((tm, d), lambda i: (i, 0)),
            out_shape=jax.ShapeDtypeStruct((m, d), out_dtype),
            compiler_params=_cparams(("parallel",)),
            name="norm_mod",
        )(*args)
    e = w_router.shape[1]
    return pl.pallas_call(
        _norm_mod_router_kernel,
        grid=(m // tm,),
        in_specs=in_specs + [pl.BlockSpec((d, e), lambda i: (0, 0))],
        out_specs=[pl.BlockSpec((tm, d), lambda i: (i, 0)),
                   pl.BlockSpec((tm, e), lambda i: (i, 0))],
        out_shape=[jax.ShapeDtypeStruct((m, d), out_dtype),
                   jax.ShapeDtypeStruct((m, e), F32)],
        compiler_params=_cparams(("parallel",)),
        name="norm_mod_router",
    )(*args, w_router)


def _final_norm_kernel(x_ref, g_ref, o_ref):
    x = x_ref[...]
    y = x * lax.rsqrt(jnp.mean(x * x, axis=-1, keepdims=True) + EPS)
    o_ref[...] = y * g_ref[...]


def final_norm(x, g):
    m, d = x.shape
    tm = _pick(m, (256, 128, 64, 32, 16, 8))
    return pl.pallas_call(
        _final_norm_kernel,
        grid=(m // tm,),
        in_specs=[pl.BlockSpec((tm, d), lambda i: (i, 0)),
                  pl.BlockSpec((1, d), lambda i: (0, 0))],
        out_specs=pl.BlockSpec((tm, d), lambda i: (i, 0)),
        out_shape=jax.ShapeDtypeStruct((m, d), F32),
        compiler_params=_cparams(("parallel",)),
        name="final_norm",
    )(x, g.reshape(1, d))


def _mm_kernel(a_ref, w_ref, o_ref):
    acc = jnp.dot(a_ref[...], w_ref[...].astype(BF16), preferred_element_type=F32)
    o_ref[...] = acc.astype(o_ref.dtype)


def _mm_bias_kernel(a_ref, w_ref, b_ref, o_ref):
    acc = jnp.dot(a_ref[...], w_ref[...].astype(BF16), preferred_element_type=F32)
    o_ref[...] = (acc + b_ref[...]).astype(o_ref.dtype)


def _mm_resid_kernel(a1_ref, a2_ref, w_ref, x_ref, gate_ref, o_ref, *, nb1):
    i = pl.program_id(0)
    w = w_ref[...].astype(BF16)

    @pl.when(i < nb1)
    def _():
        o_ref[...] = x_ref[...] + gate_ref[0] * jnp.dot(a1_ref[...], w, preferred_element_type=F32)

    @pl.when(i >= nb1)
    def _():
        o_ref[...] = x_ref[...] + gate_ref[0] * jnp.dot(a2_ref[...], w, preferred_element_type=F32)


def matmul(a, w, *, a2=None, widx=None, col0=0, ncols=None, out_dtype=F32, bias=None, resid=None,
           tm=None, tn=None):
    m1, k = a.shape
    m = m1 + (0 if a2 is None else a2.shape[0])
    n = w.shape[-1] - col0 if ncols is None else ncols
    tm = tm or _pick(m, (1024, 512, 256, 128, 64, 32, 16, 8))
    tn = tn or _pick(n, (256, 128) if resid is not None else (512, 256, 128))
    assert col0 % tn == 0 and m % tm == 0 and n % tn == 0
    cb = col0 // tn
    grid = (m // tm, n // tn)
    a_spec = pl.BlockSpec((tm, k), lambda i, j: (i, 0))
    if widx is None:
        w_spec = pl.BlockSpec((k, tn), lambda i, j: (0, cb + j))
    else:
        w_spec = pl.BlockSpec((None, k, tn), lambda i, j: (widx, 0, cb + j))
    o_spec = pl.BlockSpec((tm, tn), lambda i, j: (i, j))
    out_shape = jax.ShapeDtypeStruct((m, n), out_dtype)
    cp = _cparams(("parallel", "parallel"))
    if resid is not None:
        x, gate, n_ctx, dec_seq = resid
        assert n_ctx % tm == 0 and dec_seq % tm == 0 and a2 is not None and m1 % tm == 0
        nb1 = m1 // tm
        g_spec = pl.BlockSpec((1, 1, tn), lambda i, j: (_group_of_row(i * tm, n_ctx, dec_seq), 0, j))
        a1_spec = pl.BlockSpec((tm, k), lambda i, j: (jnp.minimum(i, nb1 - 1), 0))
        a2_spec = pl.BlockSpec((tm, k), lambda i, j: (jnp.maximum(i - nb1, 0), 0))
        return pl.pallas_call(
            functools.partial(_mm_resid_kernel, nb1=nb1), grid=grid,
            in_specs=[a1_spec, a2_spec, w_spec, o_spec, g_spec], out_specs=o_spec,
            out_shape=out_shape, compiler_params=cp, name="matmul_resid",
        )(a, a2, w, x, gate[:, None, :])
    if bias is not None:
        b_spec = pl.BlockSpec((1, tn), lambda i, j: (0, j))
        return pl.pallas_call(
            _mm_bias_kernel, grid=grid,
            in_specs=[a_spec, w_spec, b_spec], out_specs=o_spec,
            out_shape=out_shape, compiler_params=cp, name="matmul_bias",
        )(a, w, bias.reshape(1, n))
    return pl.pallas_call(
        _mm_kernel, grid=grid,
        in_specs=[a_spec, w_spec], out_specs=o_spec,
        out_shape=out_shape, compiler_params=cp, name="matmul",
    )(a, w)


def _silu(x):
    return x * (1.0 / (1.0 + jnp.exp(-x)))


def _softplus(x):
    return jnp.maximum(x, 0.0) + jnp.log1p(jnp.exp(-jnp.abs(x)))


def _split_bf16(x):
    hi = x.astype(BF16)
    lo = (x - hi.astype(F32)).astype(BF16)
    return hi, lo


def _seg_conv(x, w_ref, seg):
    r, _ = x.shape
    k = w_ref.shape[0]
    p = k // 2
    assert seg & (seg - 1) == 0
    pos = jnp.bitwise_and(lax.broadcasted_iota(jnp.int32, x.shape, 0), seg - 1)
    y = x * w_ref[p:p + 1, :]
    for j in range(k):
        off = j - p
        if off == 0:
            continue
        shifted = pltpu.roll(x, (-off) % r, axis=0)
        ok = jnp.logical_and(pos + off >= 0, pos + off < seg)
        y = y + jnp.where(ok, shifted, 0.0) * w_ref[j:j + 1, :]
    return y


def _pdot(a, b):
    acc = None
    for ia, at in enumerate(a):
        for ib, bt in enumerate(b):
            if ia + ib > 1:
                continue
            t = jnp.dot(at, bt, preferred_element_type=F32)
            acc = t if acc is None else acc + t
    return acc


def _dn_kernel(q_ref, k_ref, v_ref, z_ref, gt_ref, wq_ref, wk_ref, wv_ref, arow_ref, dtrow_ref,
               ng_ref, s0_ref, o_ref, sfin_ref,
               qn_ref, kn_ref, vn_ref, of_ref, ob_ref, *rest, seg, chunk, nheads, hps):
    s_refs = rest[:2 * hps]
    vnew_refs = rest[2 * hps:]
    l = q_ref.shape[0]
    dk = q_ref.shape[1] // hps
    c = chunk
    n = l // c
    hb = pl.program_id(1)

    def prep(i, carry):
        rows = pl.ds(pl.multiple_of(i * c, c), c)
        for hh in range(hps):
            ln = slice(hh * dk, (hh + 1) * dk)
            q = _silu(_seg_conv(q_ref[rows, ln], wq_ref.at[:, ln], seg))
            k = _silu(_seg_conv(k_ref[rows, ln], wk_ref.at[:, ln], seg))
            v = _silu(_seg_conv(v_ref[rows, ln], wv_ref.at[:, ln], seg))
            q = q * lax.rsqrt(jnp.sum(q * q, axis=-1, keepdims=True) + EPS) * (dk ** -0.5)
            k = k * lax.rsqrt(jnp.sum(k * k, axis=-1, keepdims=True) + EPS)
            qn_ref[rows, ln] = q
            kn_ref[rows, ln] = k
            vn_ref[rows, ln] = v
        return carry

    lax.fori_loop(0, n, prep, 0)

    for hh in range(hps):
        for d in range(2):
            s_refs[2 * hh + d][...] = s0_ref[0, d, hh]
            vnew_refs[2 * hh + d][...] = jnp.zeros_like(vnew_refs[2 * hh + d])

    cc = DN_CHUNK
    sh = int(math.log2(cc))
    nsub = c // cc
    ri = lax.broadcasted_iota(jnp.int32, (c, c), 0)
    ci = lax.broadcasted_iota(jnp.int32, (c, c), 1)
    same_lv = {lv: jnp.right_shift(ri, lv) == jnp.right_shift(ci, lv) for lv in range(3, sh + 1)}
    same = same_lv[sh]
    same8 = same_lv[3]
    eye = (ri == ci).astype(F32)
    incl_d = (jnp.logical_and(same, ri >= ci), jnp.logical_and(same, ri <= ci))
    strict_d = (jnp.logical_and(same, ri > ci), jnp.logical_and(same, ri < ci))
    incl_b = tuple(m.astype(BF16) for m in incl_d)
    same_b = same.astype(BF16)
    lane = lax.broadcasted_iota(jnp.int32, (c, gt_ref.shape[1]), 1)
    nt = (((1,), (1,)), ((), ()))
    tn = (((0,), (0,)), ((), ()))
    probs = [(hh, d) for hh in range(hps) for d in range(2)]

    def col(tile, idx):
        return jnp.sum(jnp.where(lane == idx, tile, 0.0), axis=-1, keepdims=True)

    def split3(x):
        t1 = x.astype(BF16)
        r = x - t1.astype(F32)
        t2 = r.astype(BF16)
        t3 = (r - t2.astype(F32)).astype(BF16)
        return t1, t2, t3

    def mask_sum(mask_b, terms):
        return (jnp.dot(mask_b, terms[0], preferred_element_type=F32)
                + jnp.dot(mask_b, terms[1], preferred_element_type=F32)
                + jnp.dot(mask_b, terms[2], preferred_element_type=F32))

    def scan(i, carry):
        blk = (i, n - 1 - i)
        rows = [pl.ds(pl.multiple_of(blk[d] * c, c), c) for d in range(2)]
        g3, sig = [], []
        for d in range(2):
            gt = gt_ref[rows[d], :]
            sig.append(1.0 / (1.0 + jnp.exp(-gt)))
            g3.append(split3(-jnp.exp(arow_ref[...]) * _softplus(gt + dtrow_ref[...])))
        gcs = [mask_sum(incl_b[d], g3[d]) for d in range(2)]
        tots = [mask_sum(same_b, g3[d]) for d in range(2)]
        st = []
        for hh, d in probs:
            h = hb * hps + hh
            gidx = 2 * nheads + d * nheads + h
            beta = col(sig[d], d * nheads + h)
            gc = col(gcs[d], gidx)
            tot = col(tots[d], gidx)
            gcb = jnp.broadcast_to(gc, (c, c))
            diff = jnp.where(incl_d[d], gcb - jnp.transpose(gcb), 0.0)
            decay = jnp.where(incl_d[d], jnp.exp(diff), 0.0)
            ln = slice(hh * dk, (hh + 1) * dk)
            q = qn_ref[rows[d], ln]
            k = kn_ref[rows[d], ln]
            v = vn_ref[rows[d], ln]
            kb = k * beta
            kbf = k.astype(BF16)
            a = lax.dot_general(kb.astype(BF16), kbf, nt, preferred_element_type=F32)
            ab = jnp.where(strict_d[d], a * decay, 0.0).astype(BF16)
            attn = (lax.dot_general(q.astype(BF16), kbf, nt, preferred_element_type=F32) * decay).astype(BF16)
            rhs = jnp.concatenate([v * beta, kb * jnp.exp(gc)], axis=-1)
            st.append(dict(ab=ab, attn=attn, rhs=rhs, qg=(q * jnp.exp(gc)).astype(BF16),
                           kd=(k * jnp.exp(tot - gc)).astype(BF16), etot=jnp.exp(tot)))
        zero = jnp.zeros((c, c), BF16)
        n8 = [jnp.where(same8, p["ab"], zero) for p in st]
        n2 = [jnp.dot(m, m, preferred_element_type=F32) for m in n8]
        n2p = [_split_bf16(m) for m in n2]
        n4 = [_pdot(m, m) for m in n2p]
        inv = [_pdot(((eye - m.astype(F32)).astype(BF16),), _split_bf16(eye + m2)) for m, m2 in zip(n8, n2)]
        inv = [_pdot(_split_bf16(m), _split_bf16(eye + m4)) for m, m4 in zip(inv, n4)]
        for lv in range(3, sh):
            pair = jnp.logical_and(same_lv[lv + 1], jnp.logical_not(same_lv[lv]))
            invp = [_split_bf16(m) for m in inv]
            t = [_pdot(mp, (jnp.where(pair, p["ab"], zero),)) for mp, p in zip(invp, st)]
            inv = [m - _pdot(_split_bf16(tt), mp) for m, tt, mp in zip(inv, t, invp)]
        xs = [_pdot(_split_bf16(m), _split_bf16(p["rhs"])) for m, p in zip(inv, st)]
        for t in range(nsub):
            for pi, (hh, d) in enumerate(probs):
                p = st[pi]
                r0 = (t if d == 0 else nsub - 1 - t) * cc
                s_ref = s_refs[pi]
                vnew_ref = vnew_refs[pi]
                s = s_ref[...]
                sb = s.astype(BF16)
                x = xs[pi]
                v_new = x[r0:r0 + cc, :dk] - jnp.dot(x[r0:r0 + cc, dk:].astype(BF16), sb,
                                                     preferred_element_type=F32)
                vnb = v_new.astype(BF16)
                vnew_ref[r0:r0 + cc, :] = vnb
                o = (jnp.dot(p["qg"][r0:r0 + cc, :], sb, preferred_element_type=F32)
                     + jnp.dot(p["attn"][r0:r0 + cc, :], vnew_ref[...], preferred_element_type=F32))
                out_ref = of_ref if d == 0 else ob_ref
                out_ref[pl.ds(pl.multiple_of(blk[d] * c + r0, cc), cc), hh * dk:(hh + 1) * dk] = o
                upd = lax.dot_general(p["kd"][r0:r0 + cc, :], vnb, tn, preferred_element_type=F32)
                s_ref[...] = s * p["etot"][r0:r0 + 1, :] + upd
        return carry

    lax.fori_loop(0, n, scan, 0)

    for hh in range(hps):
        for d in range(2):
            sfin_ref[0, d, hh] = s_refs[2 * hh + d][...]

    def fin(i, carry):
        rows = pl.ds(pl.multiple_of(i * c, c), c)
        for hh in range(hps):
            ln = slice(hh * dk, (hh + 1) * dk)
            o = of_ref[rows, ln] + ob_ref[rows, ln]
            o = o * lax.rsqrt(jnp.mean(o * o, axis=-1, keepdims=True) + EPS) * ng_ref[...]
            o_ref[rows, ln] = (o * _silu(z_ref[rows, ln].astype(F32))).astype(o_ref.dtype)
        return carry

    lax.fori_loop(0, n, fin, 0)


def deltanet_core(qkv, z, gates, w_conv_t, a_row, dt_row, norm_g, s0, *, row0, nseq, seqlen, seg, nheads, hps):
    m, d3 = qkv.shape
    d = d3 // 3
    dk = d // nheads
    assert dk == 128 and row0 % seqlen == 0 and nheads % hps == 0
    chunk = min(seqlen, 256)
    assert seqlen % chunk == 0 and chunk % seg == 0 and chunk % DN_CHUNK == 0
    rb = row0 // seqlen
    nhb = nheads // hps
    kw = w_conv_t.shape[0]
    g4 = gates.shape[1]
    wd = hps * dk

    def colspec(part):
        return pl.BlockSpec((seqlen, wd), lambda b, h: (rb + b, part * nhb + h))

    def wspec(part):
        return pl.BlockSpec((kw, wd), lambda b, h: (0, part * nhb + h))

    kern = functools.partial(_dn_kernel, seg=seg, chunk=chunk, nheads=nheads, hps=hps)
    st_spec = pl.BlockSpec((1, 2, hps, dk, dk), lambda b, h: (b, 0, h, 0, 0))
    out, sfin = pl.pallas_call(
        kern,
        grid=(nseq, nhb),
        in_specs=[colspec(0), colspec(1), colspec(2),
                  pl.BlockSpec((seqlen, wd), lambda b, h: (rb + b, h)),
                  pl.BlockSpec((seqlen, g4), lambda b, h: (rb + b, 0)),
                  wspec(0), wspec(1), wspec(2),
                  pl.BlockSpec((1, g4), lambda b, h: (0, 0)),
                  pl.BlockSpec((1, g4), lambda b, h: (0, 0)),
                  pl.BlockSpec((1, dk), lambda b, h: (0, 0)),
                  st_spec],
        out_specs=[pl.BlockSpec((seqlen, wd), lambda b, h: (b, h)), st_spec],
        out_shape=[jax.ShapeDtypeStruct((nseq * seqlen, d), BF16),
                   jax.ShapeDtypeStruct((nseq, 2, nheads, dk, dk), F32)],
        scratch_shapes=([pltpu.VMEM((seqlen, wd), F32) for _ in range(5)]
                        + [pltpu.VMEM((dk, dk), F32) for _ in range(2 * hps)]
                        + [pltpu.VMEM((chunk, dk), BF16) for _ in range(2 * hps)]),
        compiler_params=_cparams(("parallel", "parallel")),
        name="deltanet_core",
    )(qkv, qkv, qkv, z, gates, w_conv_t, w_conv_t, w_conv_t, a_row, dt_row, norm_g.reshape(1, dk), s0)
    return out, sfin


def _dot3(a_hi, a_lo, x):
    x_hi, x_lo = _split_bf16(x)
    acc = jnp.dot(a_hi, x_hi, preferred_element_type=F32)
    acc = acc + jnp.dot(a_hi, x_lo, preferred_element_type=F32)
    return acc + jnp.dot(a_lo, x_hi, preferred_element_type=F32)


def _dft_mats(p):
    n = 2 * p
    k = np.arange(p, dtype=np.float64)[:, None]
    s = np.arange(n, dtype=np.float64)[None, :]
    th = np.pi * (2.0 * k + 1.0) * s / n
    fwd = np.concatenate([np.cos(th), -np.sin(th)], axis=0)
    tht = th[:, :p].T
    inv = np.concatenate([np.cos(tht), -np.sin(tht)], axis=1) * (2.0 / n)
    return fwd, inv


def _hi_lo_const(a):
    a32 = np.asarray(a, np.float32)
    hi = a32.astype(BF16)
    lo = (a32 - hi.astype(np.float32)).astype(BF16)
    return jnp.asarray(hi), jnp.asarray(lo)


def _spec_kernel(ah_ref, al_ref, t_ref, o_ref):
    o_ref[0] = _dot3(ah_ref[...], al_ref[...], t_ref[0])


def filter_spectra(taps, p):
    nd, n, c = taps.shape
    fwd, _ = _dft_mats(p)
    fh, fl = _hi_lo_const(fwd)
    tn = _pick(c, (512, 256, 128))
    return pl.pallas_call(
        _spec_kernel,
        grid=(nd, c // tn),
        in_specs=[pl.BlockSpec((n, n), lambda d, j: (0, 0)),
                  pl.BlockSpec((n, n), lambda d, j: (0, 0)),
                  pl.BlockSpec((1, n, tn), lambda d, j: (d, 0, j))],
        out_specs=pl.BlockSpec((1, n, tn), lambda d, j: (d, 0, j)),
        out_shape=jax.ShapeDtypeStruct((nd, n, c), F32),
        compiler_params=_cparams(("parallel", "parallel")),
        name="hyena_filter_spectra",
    )(fh, fl, taps)


def _hy_kernel(x0_ref, x1_ref, v_ref, w0_ref, w1_ref, w2_ref, bias_ref, spec_ref,
               fh_ref, fl_ref, ih_ref, il_ref, o_ref, vv_ref, vh_ref, acc_ref, *, seg, p, band):
    l, tc = v_ref.shape
    nb = l // p

    def fwd(j, carry):
        rows = pl.ds(pl.multiple_of(j * p, p), p)
        vv = _seg_conv(v_ref[rows, :], w2_ref, seg) * _seg_conv(x1_ref[rows, :], w1_ref, seg)
        vv_ref[rows, :] = vv
        vh_ref[j] = _dot3(fh_ref[...], fl_ref[...], vv)
        return carry

    lax.fori_loop(0, nb, fwd, 0)

    rt = 64

    def out_block(i, carry):
        def row_tile(r, c2):
            rr = pl.ds(pl.multiple_of(r * rt, rt), rt)
            ri = pl.ds(pl.multiple_of(p + r * rt, rt), rt)

            j0 = (i // band) * band

            def mac(jj, acc):
                are, aim = acc
                j = j0 + jj
                d = i - j + (band - 1)
                fre = spec_ref[d, rr, :]
                fim = spec_ref[d, ri, :]
                xre = vh_ref[j, rr, :]
                xim = vh_ref[j, ri, :]
                return (are + fre * xre - fim * xim, aim + fre * xim + fim * xre)

            zero = jnp.zeros((rt, tc), F32)
            are, aim = lax.fori_loop(0, band, mac, (zero, zero))
            acc_ref[rr, :] = are
            acc_ref[ri, :] = aim
            return c2

        lax.fori_loop(0, p // rt, row_tile, 0)
        y = _dot3(ih_ref[...], il_ref[...], acc_ref[...])
        rows = pl.ds(pl.multiple_of(i * p, p), p)
        y = y + vv_ref[rows, :] * bias_ref[...]
        o_ref[rows, :] = (y * _seg_conv(x0_ref[rows, :], w0_ref, seg)).astype(o_ref.dtype)
        return carry

    lax.fori_loop(0, nb, out_block, 0)


def hyena_core(u, w_conv_t, bias, spec, *, row0, nseq, seqlen, seg, p, spb=1):
    m, c3 = u.shape
    c = c3 // 3
    tc = 128
    assert c % tc == 0 and seqlen % p == 0 and p % seg == 0 and nseq % spb == 0
    band = seqlen // p
    assert spec.shape[0] == 2 * band - 1
    nseq, seqlen = nseq // spb, seqlen * spb
    assert row0 % seqlen == 0
    nct = c // tc
    rb = row0 // seqlen
    nd = spec.shape[0]
    kw = w_conv_t.shape[0]
    fwd, inv = _dft_mats(p)
    fh, fl = _hi_lo_const(fwd[:, :p])
    ih, il = _hi_lo_const(inv)

    def colspec(part):
        return pl.BlockSpec((seqlen, tc), lambda j, b: (rb + b, part * nct + j))

    def wspec(part):
        return pl.BlockSpec((kw, tc), lambda j, b: (0, part * nct + j))

    const = lambda shape: pl.BlockSpec(shape, lambda j, b: (0,) * len(shape))
    kern = functools.partial(_hy_kernel, seg=seg, p=p, band=band)
    return pl.pallas_call(
        kern,
        grid=(nct, nseq),
        in_specs=[colspec(0), colspec(1), colspec(2), wspec(0), wspec(1), wspec(2),
                  pl.BlockSpec((1, tc), lambda j, b: (0, j)),
                  pl.BlockSpec((nd, 2 * p, tc), lambda j, b: (0, 0, j)),
                  const((2 * p, p)), const((2 * p, p)), const((p, 2 * p)), const((p, 2 * p))],
        out_specs=pl.BlockSpec((seqlen, tc), lambda j, b: (b, j)),
        out_shape=jax.ShapeDtypeStruct((nseq * seqlen, c), BF16),
        scratch_shapes=[pltpu.VMEM((seqlen, tc), F32),
                        pltpu.VMEM((seqlen // p, 2 * p, tc), F32),
                        pltpu.VMEM((2 * p, tc), F32)],
        compiler_params=_cparams(("parallel", "parallel")),
        name="hyena_core",
    )(u, u, u, w_conv_t, w_conv_t, w_conv_t, bias.reshape(1, c), spec, fh, fl, ih, il)


def _filter_kernel(feat_ref, w1_ref, b1_ref, w2_ref, b2_ref, f_ref, w3_ref, dec_ref, o_ref):
    feat = feat_ref[...]
    hid = jnp.sin(f_ref[0:1, :] * (jnp.dot(feat, w1_ref[...], precision=HI, preferred_element_type=F32)
                                  + b1_ref[...]))
    hid = jnp.sin(f_ref[1:2, :] * (jnp.dot(hid, w2_ref[...], precision=HI, preferred_element_type=F32)
                                  + b2_ref[...]))
    h = jnp.dot(hid, w3_ref[...], precision=HI, preferred_element_type=F32)
    o_ref[...] = h * jnp.exp(-feat[:, 0:1] * jnp.abs(dec_ref[...]))


def hyena_filter_taps(l, p, w1, b1, w2, b2, w3, freq, decay):
    t = np.linspace(0.0, 1.0, l, dtype=np.float32)[:, None]
    bands = np.linspace(1e-4, HY_BANDS - 1, HY_BANDS, dtype=np.float32)
    ang = (np.float32(2.0 * math.pi / l) * np.arange(l, dtype=np.float32)[:, None] * bands).astype(np.float32)
    feat = np.concatenate([t, np.cos(ang), -np.sin(ang)], axis=-1).astype(np.float32)
    kf = 128
    nfeat, hidden = w1.shape
    c2 = w3.shape[1]
    feat = jnp.asarray(np.pad(feat, ((0, 0), (0, kf - nfeat))))
    w1p = jnp.pad(w1, ((0, kf - nfeat), (0, 0)))
    tcn = _pick(c2, (1024, 512, 256, 128))
    const = lambda shape: pl.BlockSpec(shape, lambda j: (0,) * len(shape))
    h = pl.pallas_call(
        _filter_kernel,
        grid=(c2 // tcn,),
        in_specs=[const((l, kf)), const((kf, hidden)), const((1, hidden)), const((hidden, hidden)),
                  const((1, hidden)), const((2, hidden)),
                  pl.BlockSpec((hidden, tcn), lambda j: (0, j)),
                  pl.BlockSpec((1, tcn), lambda j: (0, j))],
        out_specs=pl.BlockSpec((l, tcn), lambda j: (0, j)),
        out_shape=jax.ShapeDtypeStruct((l, c2), F32),
        compiler_params=_cparams(("parallel",)),
        name="hyena_filter",
    )(feat, w1p, b1[None, :], w2, b2[None, :], freq, w3, decay.reshape(1, c2))
    h_fwd, h_bwd = h[:, :c2 // 2], h[:, c2 // 2:]
    denom = (jnp.sum(jnp.abs(h_fwd), axis=0, keepdims=True)
             + jnp.sum(jnp.abs(h_bwd[1:]), axis=0, keepdims=True) + EPS)
    hcat = jnp.concatenate([h_fwd, h_bwd], axis=0) / denom
    nb = l // p
    n = 2 * p
    mi = np.arange(n)
    delta = np.where(mi < p, mi, mi - n)
    sign = np.where(mi < p, 1.0, -1.0) * (mi != p)
    off = np.arange(-(nb - 1), nb)[:, None] * p + delta[None, :]
    valid = np.abs(off) < l
    idx = np.where(valid, np.where(off >= 0, off, l - off), 0).astype(np.int32)
    coef = (valid * sign[None, :]).astype(np.float32)
    return hcat[jnp.asarray(idx)] * jnp.asarray(coef)[:, :, None]


def _gather_kernel(idx_ref, src_ref, o_ref, stage_ref, sem):
    i = pl.program_id(0)
    tg = stage_ref.shape[0]

    def issue(r, carry):
        row = idx_ref[i * tg + r]
        pltpu.make_async_copy(src_ref.at[pl.ds(row, 1)], stage_ref.at[pl.ds(r, 1)], sem).start()
        return carry

    lax.fori_loop(0, tg, issue, 0)

    def drain(r, carry):
        pltpu.make_async_copy(src_ref.at[pl.ds(0, 1)], stage_ref.at[pl.ds(r, 1)], sem).wait()
        return carry

    lax.fori_loop(0, tg, drain, 0)
    o_ref[...] = stage_ref[...].astype(o_ref.dtype)


def gather_rows(src, idx, tg=512):
    npick = idx.shape[0]
    d = src.shape[1]
    assert npick % tg == 0
    return pl.pallas_call(
        _gather_kernel,
        grid_spec=pltpu.PrefetchScalarGridSpec(
            num_scalar_prefetch=1,
            grid=(npick // tg,),
            in_specs=[pl.BlockSpec(memory_space=pl.ANY)],
            out_specs=pl.BlockSpec((tg, d), lambda i, idx_ref: (i, 0)),
            scratch_shapes=[pltpu.VMEM((tg, d), F32), pltpu.SemaphoreType.DMA(())],
        ),
        out_shape=jax.ShapeDtypeStruct((npick, d), BF16),
        compiler_params=_cparams(("arbitrary",)),
        name="moe_gather",
    )(idx, src)


def _ffn_kernel(xs_ref, wg_ref, wu_ref, wd_ref, o_ref, hid_ref, *, nf, tf):
    j = pl.program_id(1)

    @pl.when(j < nf)
    def _():
        xs = xs_ref[...]
        g = jnp.dot(xs, wg_ref[0].astype(BF16), preferred_element_type=F32)
        u = jnp.dot(xs, wu_ref[0].astype(BF16), preferred_element_type=F32)
        hid_ref[j] = (_silu(g) * u).astype(BF16)

    @pl.when(j >= nf)
    def _():
        acc = jnp.dot(hid_ref[0], wd_ref[0, 0:tf, :].astype(BF16), preferred_element_type=F32)
        for f in range(1, nf):
            acc = acc + jnp.dot(hid_ref[f], wd_ref[0, f * tf:(f + 1) * tf, :].astype(BF16),
                                preferred_element_type=F32)
        o_ref[...] = acc


def expert_ffn(xs, w_gate, w_up, w_down, lyr, cap):
    npick, d = xs.shape
    _, e, _, ff = w_gate.shape
    nse = npick // cap
    tf = _pick(ff, (256, 128))
    td = _pick(d, (512, 256, 128))
    nf, ndt = ff // tf, d // td
    kern = functools.partial(_ffn_kernel, nf=nf, tf=tf)
    return pl.pallas_call(
        kern,
        grid=(nse, nf + ndt),
        in_specs=[pl.BlockSpec((cap, d), lambda s, j: (s, 0)),
                  pl.BlockSpec((None, 1, d, tf), lambda s, j: (lyr, s % e, 0, jnp.minimum(j, nf - 1))),
                  pl.BlockSpec((None, 1, d, tf), lambda s, j: (lyr, s % e, 0, jnp.minimum(j, nf - 1))),
                  pl.BlockSpec((None, 1, ff, td), lambda s, j: (lyr, s % e, 0, jnp.maximum(j - nf, 0)))],
        out_specs=pl.BlockSpec((cap, td), lambda s, j: (s, jnp.maximum(j - nf, 0))),
        out_shape=jax.ShapeDtypeStruct((npick, d), F32),
        scratch_shapes=[pltpu.VMEM((nf, cap, tf), BF16)],
        compiler_params=_cparams(("parallel", "arbitrary")),
        name="moe_expert_ffn",
    )(xs, w_gate, w_up, w_down)


def _scatter_kernel(idx_ref, ye_ref, coef_ref, sel_ref, ga_ref, gb_ref, x_in_ref, x_ref,
                    stage_ref, sem_r, sem_w):
    del x_in_ref
    i = pl.program_id(0)
    tg = stage_ref.shape[0]

    def rd(r, carry):
        row = idx_ref[i * tg + r]
        pltpu.make_async_copy(x_ref.at[pl.ds(row, 1)], stage_ref.at[pl.ds(r, 1)], sem_r).start()
        return carry

    lax.fori_loop(0, tg, rd, 0)

    def rd_wait(r, carry):
        pltpu.make_async_copy(x_ref.at[pl.ds(0, 1)], stage_ref.at[pl.ds(r, 1)], sem_r).wait()
        return carry

    lax.fori_loop(0, tg, rd_wait, 0)
    gvec = jnp.where(sel_ref[...] > 0.5, gb_ref[0], ga_ref[0])
    stage_ref[...] = stage_ref[...] + gvec * (coef_ref[...] * ye_ref[...])

    def wr(r, carry):
        row = idx_ref[i * tg + r]
        pltpu.make_async_copy(stage_ref.at[pl.ds(r, 1)], x_ref.at[pl.ds(row, 1)], sem_w).start()
        return carry

    lax.fori_loop(0, tg, wr, 0)

    def wr_wait(r, carry):
        pltpu.make_async_copy(stage_ref.at[pl.ds(r, 1)], x_ref.at[pl.ds(0, 1)], sem_w).wait()
        return carry

    lax.fori_loop(0, tg, wr_wait, 0)


def scatter_add_rows(x, ye, idx, coef, sel, gab, picks_per_stream, tg=512):
    npick, d = ye.shape
    assert npick % tg == 0 and picks_per_stream % tg == 0
    cps = picks_per_stream // tg
    return pl.pallas_call(
        _scatter_kernel,
        grid_spec=pltpu.PrefetchScalarGridSpec(
            num_scalar_prefetch=1,
            grid=(npick // tg,),
            in_specs=[pl.BlockSpec((tg, d), lambda i, idx_ref: (i, 0)),
                      pl.BlockSpec((tg, 1), lambda i, idx_ref: (i, 0)),
                      pl.BlockSpec((tg, 1), lambda i, idx_ref: (i, 0)),
                      pl.BlockSpec((1, 1, d), lambda i, idx_ref: (2 * (i // cps), 0, 0)),
                      pl.BlockSpec((1, 1, d), lambda i, idx_ref: (2 * (i // cps) + 1, 0, 0)),
                      pl.BlockSpec(memory_space=pl.ANY)],
            out_specs=pl.BlockSpec(memory_space=pl.ANY),
            scratch_shapes=[pltpu.VMEM((tg, d), F32), pltpu.SemaphoreType.DMA(()),
                            pltpu.SemaphoreType.DMA(())],
        ),
        out_shape=jax.ShapeDtypeStruct(x.shape, x.dtype),
        input_output_aliases={6: 0},
        compiler_params=_cparams(("arbitrary",)),
        name="moe_scatter_add",
    )(idx, ye, coef, sel, gab, gab, x)


def moe_layer(x, hn, aff, g2, w_gate, w_up, w_down, lyr, n_ctx, dec_seq):
    m, d = x.shape
    e = aff.shape[1]
    n_dec = m - n_ctx
    assert n_ctx == n_dec, "merged MoE kernels need equal routed-set sizes"
    cap = EC_FACTOR * n_ctx // e
    gate0, idx0 = lax.top_k(aff[:n_ctx].T, cap)
    gate1, idx1 = lax.top_k(aff[n_ctx:].T, cap)
    idx = jnp.concatenate([idx0.reshape(-1), idx1.reshape(-1) + n_ctx]).astype(jnp.int32)
    coef = jnp.concatenate([gate0.reshape(-1), gate1.reshape(-1)])[:, None]
    sel = (idx >= n_ctx + dec_seq).astype(F32)[:, None]
    ngrp = g2.shape[0]
    gab = jnp.stack([g2[0], g2[0], g2[1], g2[min(2, ngrp - 1)]])[:, None, :]
    xs = gather_rows(hn, idx)
    ye = expert_ffn(xs, w_gate, w_up, w_down, lyr, cap)
    return scatter_add_rows(x, ye, idx, coef, sel, gab, e * cap)


def kernel(x_prompt, x_sample, state_dn, c, c_ctx, ada_w, ada_b, norm1_g, norm2_g, dn_w_in, dn_conv, dn_a_log, dn_dt_bias, dn_norm_g, dn_w_out, hy_w_in, hy_conv, hy_w1, hy_b1, hy_w2, hy_b2, hy_w3, hy_freq, hy_decay, hy_bias, hy_w_out, moe_router, moe_w_gate, moe_w_up, moe_w_down, final_g):
    bsz, seq, d = x_prompt.shape
    dbsz, dseq, _ = x_sample.shape
    assert dbsz == 2, "scatter group select handles two latent requests"
    depth = ada_w.shape[0]
    nheads = dn_a_log.shape[2]
    n_ctx = bsz * seq
    n_dec = dbsz * dseq
    x = jnp.concatenate([x_prompt.reshape(n_ctx, d), x_sample.reshape(n_dec, d)], axis=0)

    cvec = jnp.concatenate([c_ctx[None, :], c], axis=0)
    ngrp = cvec.shape[0]
    sc_in = jnp.zeros((16, d), F32).at[:ngrp].set(cvec)
    sc_in = (sc_in * (1.0 / (1.0 + jnp.exp(-sc_in)))).astype(BF16)

    p_ctx = min(seq, 512)
    p_dec = min(dseq, 512)
    states = []
    for i in range(depth):
        mod = matmul(sc_in, ada_w, widx=i, bias=ada_b[i], tm=16)[:ngrp]
        sh1, sc1, g1, sh2, sc2, g2 = [mod[:, t * d:(t + 1) * d] for t in range(N_MOD)]
        hn = norm_mod(x, norm1_g[i], sc1, sh1, n_ctx, dseq, BF16)
        j = i // 2
        if i % 2 == 0:
            qkv = matmul(hn, dn_w_in, widx=j, col0=0, ncols=3 * d)
            z = matmul(hn, dn_w_in, widx=j, col0=3 * d, ncols=d, out_dtype=BF16)
            gates = matmul(hn, dn_w_in, widx=j, col0=4 * d, ncols=4 * nheads, tn=4 * nheads)
            wct = dn_conv[j].T
            a_row = jnp.concatenate([jnp.zeros((2 * nheads,), F32), dn_a_log[j].reshape(-1)])[None, :]
            dt_row = jnp.concatenate([jnp.zeros((2 * nheads,), F32), dn_dt_bias[j].reshape(-1)])[None, :]
            dk = d // nheads
            o_ctx, s_ctx = deltanet_core(qkv, z, gates, wct, a_row, dt_row, dn_norm_g[j],
                                         jnp.zeros((bsz, 2, nheads, dk, dk), F32),
                                         row0=0, nseq=bsz, seqlen=seq, seg=seq, nheads=nheads, hps=2)
            o_dec, _ = deltanet_core(qkv, z, gates, wct, a_row, dt_row, dn_norm_g[j], state_dn[:, j],
                                     row0=n_ctx, nseq=dbsz, seqlen=dseq, seg=GRID_W, nheads=nheads, hps=1)
            states.append(s_ctx)
            x = matmul(o_ctx, dn_w_out, a2=o_dec, widx=j, resid=(x, g1, n_ctx, dseq))
        else:
            u = matmul(hn, hy_w_in, widx=j)
            wct = hy_conv[j].T
            filt = (hy_w1[j], hy_b1[j], hy_w2[j], hy_b2[j], hy_w3[j], hy_freq[j], hy_decay[j])
            spec_ctx = filter_spectra(hyena_filter_taps(seq, p_ctx, *filt), p_ctx)
            spec_dec = filter_spectra(hyena_filter_taps(dseq, p_dec, *filt), p_dec)
            y_ctx = hyena_core(u, wct, hy_bias[j], spec_ctx, row0=0, nseq=bsz, seqlen=seq, seg=seq, p=p_ctx,
                               spb=_pick(bsz, (8, 4, 2)))
            y_dec = hyena_core(u, wct, hy_bias[j], spec_dec, row0=n_ctx, nseq=dbsz, seqlen=dseq,
                               seg=GRID_W, p=p_dec)
            x = matmul(y_ctx, hy_w_out, a2=y_dec, widx=j, resid=(x, g1, n_ctx, dseq))
        hn2, aff = norm_mod(x, norm2_g[i], sc2, sh2, n_ctx, dseq, F32, w_router=moe_router[i])
        x = moe_layer(x, hn2, aff, g2, moe_w_gate, moe_w_up, moe_w_down, i, n_ctx, dseq)
    y = final_norm(x, final_g)
    y_prompt = y[:n_ctx].reshape(bsz, seq, d)
    y_sample = y[n_ctx:].reshape(dbsz, dseq, d)
    return (y_prompt, y_sample, jnp.stack(states, axis=1))
```

```python
import functools
import math

import jax
import jax.numpy as jnp
import numpy as np
from jax import lax
from jax.experimental import pallas as pl
from jax.experimental.pallas import tpu as pltpu

EPS = 1e-6
N_MOD = 6
GRID_W = 64
DN_CHUNK = 64
EC_FACTOR = 2
HY_BANDS = 16
V7X_VMEM_LIMIT = 56 * 1024 * 1024

F32 = jnp.float32
BF16 = jnp.bfloat16
HI = lax.Precision.HIGHEST


def _cparams(sem):
    return pltpu.CompilerParams(dimension_semantics=sem, vmem_limit_bytes=V7X_VMEM_LIMIT)


def _group_of_row(r0, n_ctx, dec_seq):
    return jnp.where(r0 < n_ctx, 0, 1 + (r0 - n_ctx) // dec_seq)


def _pick(n, cands):
    for c in cands:
        if n % c == 0:
            return c
    return n


def _norm_mod_kernel(x_ref, g_ref, sc_ref, sh_ref, o_ref):
    x = x_ref[...]
    y = x * lax.rsqrt(jnp.mean(x * x, axis=-1, keepdims=True) + EPS)
    y = y * g_ref[...]
    o_ref[...] = (y * (1.0 + sc_ref[0]) + sh_ref[0]).astype(o_ref.dtype)


def _norm_mod_router_kernel(x_ref, g_ref, sc_ref, sh_ref, wr_ref, o_ref, aff_ref):
    x = x_ref[...]
    y = x * lax.rsqrt(jnp.mean(x * x, axis=-1, keepdims=True) + EPS)
    y = y * g_ref[...]
    hn = y * (1.0 + sc_ref[0]) + sh_ref[0]
    o_ref[...] = hn.astype(o_ref.dtype)
    logits = jnp.dot(hn, wr_ref[...], precision=HI, preferred_element_type=F32)
    m = jnp.max(logits, axis=-1, keepdims=True)
    e = jnp.exp(logits - m)
    aff_ref[...] = e / jnp.sum(e, axis=-1, keepdims=True)


def norm_mod(x, g, sc, sh, n_ctx, dec_seq, out_dtype, w_router=None):
    m, d = x.shape
    tm = _pick(m, (256, 128, 64, 32, 16, 8))
    assert n_ctx % tm == 0 and dec_seq % tm == 0
    grp = lambda i: (_group_of_row(i * tm, n_ctx, dec_seq), 0, 0)
    in_specs = [
        pl.BlockSpec((tm, d), lambda i: (i, 0)),
        pl.BlockSpec((1, d), lambda i: (0, 0)),
        pl.BlockSpec((1, 1, d), grp),
        pl.BlockSpec((1, 1, d), grp),
    ]
    args = [x, g.reshape(1, d), sc[:, None, :], sh[:, None, :]]
    if w_router is None:
        return pl.pallas_call(
            _norm_mod_kernel,
            grid=(m // tm,),
            in_specs=in_specs,
            out_specs=pl.BlockSpec((tm, d), lambda i: (i, 0)),
            out_shape=jax.ShapeDtypeStruct((m, d), out_dtype),
            compiler_params=_cparams(("parallel",)),
            name="norm_mod",
        )(*args)
    e = w_router.shape[1]
    return pl.pallas_call(
        _norm_mod_router_kernel,
        grid=(m // tm,),
        in_specs=in_specs + [pl.BlockSpec((d, e), lambda i: (0, 0))],
        out_specs=[pl.BlockSpec((tm, d), lambda i: (i, 0)),
                   pl.BlockSpec((tm, e), lambda i: (i, 0))],
        out_shape=[jax.ShapeDtypeStruct((m, d), out_dtype),
                   jax.ShapeDtypeStruct((m, e), F32)],
        compiler_params=_cparams(("parallel",)),
        name="norm_mod_router",
    )(*args, w_router)


def _final_norm_kernel(x_ref, g_ref, o_ref):
    x = x_ref[...]
    y = x * lax.rsqrt(jnp.mean(x * x, axis=-1, keepdims=True) + EPS)
    o_ref[...] = y * g_ref[...]


def final_norm(x, g):
    m, d = x.shape
    tm = _pick(m, (256, 128, 64, 32, 16, 8))
    return pl.pallas_call(
        _final_norm_kernel,
        grid=(m // tm,),
        in_specs=[pl.BlockSpec((tm, d), lambda i: (i, 0)),
                  pl.BlockSpec((1, d), lambda i: (0, 0))],
        out_specs=pl.BlockSpec((tm, d), lambda i: (i, 0)),
        out_shape=jax.ShapeDtypeStruct((m, d), F32),
        compiler_params=_cparams(("parallel",)),
        name="final_norm",
    )(x, g.reshape(1, d))


def _mm_kernel(a_ref, w_ref, o_ref):
    acc = jnp.dot(a_ref[...], w_ref[...].astype(BF16), preferred_element_type=F32)
    o_ref[...] = acc.astype(o_ref.dtype)


def _mm_bias_kernel(a_ref, w_ref, b_ref, o_ref):
    acc = jnp.dot(a_ref[...], w_ref[...].astype(BF16), preferred_element_type=F32)
    o_ref[...] = (acc + b_ref[...]).astype(o_ref.dtype)


def _mm_resid_kernel(a1_ref, a2_ref, w_ref, x_ref, gate_ref, o_ref, *, nb1):
    i = pl.program_id(0)
    w = w_ref[...].astype(BF16)

    @pl.when(i < nb1)
    def _():
        o_ref[...] = x_ref[...] + gate_ref[0] * jnp.dot(a1_ref[...], w, preferred_element_type=F32)

    @pl.when(i >= nb1)
    def _():
        o_ref[...] = x_ref[...] + gate_ref[0] * jnp.dot(a2_ref[...], w, preferred_element_type=F32)


def matmul(a, w, *, a2=None, widx=None, col0=0, ncols=None, out_dtype=F32, bias=None, resid=None,
           tm=None, tn=None):
    m1, k = a.shape
    m = m1 + (0 if a2 is None else a2.shape[0])
    n = w.shape[-1] - col0 if ncols is None else ncols
    tm = tm or _pick(m, (1024, 512, 256, 128, 64, 32, 16, 8))
    tn = tn or _pick(n, (256, 128) if resid is not None else (512, 256, 128))
    assert col0 % tn == 0 and m % tm == 0 and n % tn == 0
    cb = col0 // tn
    grid = (m // tm, n // tn)
    a_spec = pl.BlockSpec((tm, k), lambda i, j: (i, 0))
    if widx is None:
        w_spec = pl.BlockSpec((k, tn), lambda i, j: (0, cb + j))
    else:
        w_spec = pl.BlockSpec((None, k, tn), lambda i, j: (widx, 0, cb + j))
    o_spec = pl.BlockSpec((tm, tn), lambda i, j: (i, j))
    out_shape = jax.ShapeDtypeStruct((m, n), out_dtype)
    cp = _cparams(("parallel", "parallel"))
    if resid is not None:
        x, gate, n_ctx, dec_seq = resid
        assert n_ctx % tm == 0 and dec_seq % tm == 0 and a2 is not None and m1 % tm == 0
        nb1 = m1 // tm
        g_spec = pl.BlockSpec((1, 1, tn), lambda i, j: (_group_of_row(i * tm, n_ctx, dec_seq), 0, j))
        a1_spec = pl.BlockSpec((tm, k), lambda i, j: (jnp.minimum(i, nb1 - 1), 0))
        a2_spec = pl.BlockSpec((tm, k), lambda i, j: (jnp.maximum(i - nb1, 0), 0))
        return pl.pallas_call(
            functools.partial(_mm_resid_kernel, nb1=nb1), grid=grid,
            in_specs=[a1_spec, a2_spec, w_spec, o_spec, g_spec], out_specs=o_spec,
            out_shape=out_shape, compiler_params=cp, name="matmul_resid",
        )(a, a2, w, x, gate[:, None, :])
    if bias is not None:
        b_spec = pl.BlockSpec((1, tn), lambda i, j: (0, j))
        return pl.pallas_call(
            _mm_bias_kernel, grid=grid,
            in_specs=[a_spec, w_spec, b_spec], out_specs=o_spec,
            out_shape=out_shape, compiler_params=cp, name="matmul_bias",
        )(a, w, bias.reshape(1, n))
    return pl.pallas_call(
        _mm_kernel, grid=grid,
        in_specs=[a_spec, w_spec], out_specs=o_spec,
        out_shape=out_shape, compiler_params=cp, name="matmul",
    )(a, w)


def _silu(x):
    return x * (1.0 / (1.0 + jnp.exp(-x)))


def _softplus(x):
    return jnp.maximum(x, 0.0) + jnp.log1p(jnp.exp(-jnp.abs(x)))


def _split_bf16(x):
    hi = x.astype(BF16)
    lo = (x - hi.astype(F32)).astype(BF16)
    return hi, lo


def _seg_conv(x, w_ref, seg):
    r, _ = x.shape
    k = w_ref.shape[0]
    p = k // 2
    assert seg & (seg - 1) == 0
    pos = jnp.bitwise_and(lax.broadcasted_iota(jnp.int32, x.shape, 0), seg - 1)
    y = x * w_ref[p:p + 1, :]
    for j in range(k):
        off = j - p
        if off == 0:
            continue
        shifted = pltpu.roll(x, (-off) % r, axis=0)
        ok = jnp.logical_and(pos + off >= 0, pos + off < seg)
        y = y + jnp.where(ok, shifted, 0.0) * w_ref[j:j + 1, :]
    return y


def _pdot(a, b):
    acc = None
    for ia, at in enumerate(a):
        for ib, bt in enumerate(b):
            if ia + ib > 1:
                continue
            t = jnp.dot(at, bt, preferred_element_type=F32)
            acc = t if acc is None else acc + t
    return acc


def _dn_kernel(q_ref, k_ref, v_ref, z_ref, gt_ref, wq_ref, wk_ref, wv_ref, arow_ref, dtrow_ref,
               ng_ref, s0_ref, o_ref, sfin_ref,
               qn_ref, kn_ref, vn_ref, of_ref, ob_ref, *rest, seg, chunk, nheads, hps, unroll):
    s_refs = rest[:2 * hps]
    vnew_refs = rest[2 * hps:]
    l = q_ref.shape[0]
    dk = q_ref.shape[1] // hps
    c = chunk
    n = l // c
    hb = pl.program_id(1)

    def prep(i, carry):
        rows = pl.ds(pl.multiple_of(i * c, c), c)
        for hh in range(hps):
            ln = slice(hh * dk, (hh + 1) * dk)
            q = _silu(_seg_conv(q_ref[rows, ln], wq_ref.at[:, ln], seg))
            k = _silu(_seg_conv(k_ref[rows, ln], wk_ref.at[:, ln], seg))
            v = _silu(_seg_conv(v_ref[rows, ln], wv_ref.at[:, ln], seg))
            q = q * lax.rsqrt(jnp.sum(q * q, axis=-1, keepdims=True) + EPS) * (dk ** -0.5)
            k = k * lax.rsqrt(jnp.sum(k * k, axis=-1, keepdims=True) + EPS)
            qn_ref[rows, ln] = q
            kn_ref[rows, ln] = k
            vn_ref[rows, ln] = v
        return carry

    lax.fori_loop(0, n, prep, 0)

    for hh in range(hps):
        for d in range(2):
            s_refs[2 * hh + d][...] = s0_ref[0, d, hh]

    for r in vnew_refs:
        r[...] = jnp.zeros_like(r)

    cc = DN_CHUNK
    sh = int(math.log2(cc))
    nsub = c // cc
    ri = lax.broadcasted_iota(jnp.int32, (c, c), 0)
    ci = lax.broadcasted_iota(jnp.int32, (c, c), 1)
    same_lv = {lv: jnp.right_shift(ri, lv) == jnp.right_shift(ci, lv) for lv in range(3, sh + 1)}
    same = same_lv[sh]
    same8 = same_lv[3]
    eye = (ri == ci).astype(F32)
    incl_d = (jnp.logical_and(same, ri >= ci), jnp.logical_and(same, ri <= ci))
    strict_d = (jnp.logical_and(same, ri > ci), jnp.logical_and(same, ri < ci))
    incl_b = tuple(m.astype(BF16) for m in incl_d)
    same_b = same.astype(BF16)
    lane = lax.broadcasted_iota(jnp.int32, (c, gt_ref.shape[1]), 1)
    nt = (((1,), (1,)), ((), ()))
    tn = (((0,), (0,)), ((), ()))
    probs = [(hh, d, u) for u in range(unroll) for hh in range(hps) for d in range(2)]

    def col(tile, idx):
        return jnp.sum(jnp.where(lane == idx, tile, 0.0), axis=-1, keepdims=True)

    def split3(x):
        t1 = x.astype(BF16)
        r = x - t1.astype(F32)
        t2 = r.astype(BF16)
        t3 = (r - t2.astype(F32)).astype(BF16)
        return t1, t2, t3

    def mask_sum(mask_b, terms):
        return (jnp.dot(mask_b, terms[0], preferred_element_type=F32)
                + jnp.dot(mask_b, terms[1], preferred_element_type=F32)
                + jnp.dot(mask_b, terms[2], preferred_element_type=F32))

    def scan(i, carry):
        blk = [[i * unroll + u for u in range(unroll)], [n - 1 - (i * unroll + u) for u in range(unroll)]]
        rows = [[pl.ds(pl.multiple_of(blk[d][u] * c, c), c) for u in range(unroll)] for d in range(2)]
        sig, gcs, tots = {}, {}, {}
        for u in range(unroll):
            for d in range(2):
                gt = gt_ref[rows[d][u], :]
                sig[d, u] = 1.0 / (1.0 + jnp.exp(-gt))
                g3 = split3(-jnp.exp(arow_ref[...]) * _softplus(gt + dtrow_ref[...]))
                gcs[d, u] = mask_sum(incl_b[d], g3)
                tots[d, u] = mask_sum(same_b, g3)
        st = []
        for hh, d, u in probs:
            h = hb * hps + hh
            gidx = 2 * nheads + d * nheads + h
            beta = col(sig[d, u], d * nheads + h)
            gc = col(gcs[d, u], gidx)
            tot = col(tots[d, u], gidx)
            gcb = jnp.broadcast_to(gc, (c, c))
            diff = jnp.where(incl_d[d], gcb - jnp.transpose(gcb), 0.0)
            decay = jnp.where(incl_d[d], jnp.exp(diff), 0.0)
            ln = slice(hh * dk, (hh + 1) * dk)
            q = qn_ref[rows[d][u], ln]
            k = kn_ref[rows[d][u], ln]
            v = vn_ref[rows[d][u], ln]
            kb = k * beta
            kbf = k.astype(BF16)
            a = lax.dot_general(kb.astype(BF16), kbf, nt, preferred_element_type=F32)
            ab = jnp.where(strict_d[d], a * decay, 0.0).astype(BF16)
            attn = (lax.dot_general(q.astype(BF16), kbf, nt, preferred_element_type=F32) * decay).astype(BF16)
            rhs = jnp.concatenate([v * beta, kb * jnp.exp(gc)], axis=-1)
            st.append(dict(ab=ab, attn=attn, rhs=rhs, qg=(q * jnp.exp(gc)).astype(BF16),
                           kd=(k * jnp.exp(tot - gc)).astype(BF16), etot=jnp.exp(tot)))
        zero = jnp.zeros((c, c), BF16)
        n8 = [jnp.where(same8, p["ab"], zero) for p in st]
        n2 = [jnp.dot(m, m, preferred_element_type=F32) for m in n8]
        n2p = [_split_bf16(m) for m in n2]
        n4 = [_pdot(m, m) for m in n2p]
        inv = [_pdot(((eye - m.astype(F32)).astype(BF16),), _split_bf16(eye + m2)) for m, m2 in zip(n8, n2)]
        inv = [_pdot(_split_bf16(m), _split_bf16(eye + m4)) for m, m4 in zip(inv, n4)]
        for lv in range(3, sh):
            pair = jnp.logical_and(same_lv[lv + 1], jnp.logical_not(same_lv[lv]))
            invp = [_split_bf16(m) for m in inv]
            t = [_pdot(mp, (jnp.where(pair, p["ab"], zero),)) for mp, p in zip(invp, st)]
            inv = [m - _pdot(_split_bf16(tt), mp) for m, tt, mp in zip(inv, t, invp)]
        xs = [_pdot(_split_bf16(m), _split_bf16(p["rhs"])) for m, p in zip(inv, st)]
        for u in range(unroll):
            for t in range(nsub):
                for pi, (hh, d, uu) in enumerate(probs):
                    if uu != u:
                        continue
                    p = st[pi]
                    r0 = (t if d == 0 else nsub - 1 - t) * cc
                    s_ref = s_refs[2 * hh + d]
                    vnew_ref = vnew_refs[pi]
                    s = s_ref[...]
                    sb = s.astype(BF16)
                    x = xs[pi]
                    v_new = x[r0:r0 + cc, :dk] - jnp.dot(x[r0:r0 + cc, dk:].astype(BF16), sb,
                                                         preferred_element_type=F32)
                    vnb = v_new.astype(BF16)
                    vnew_ref[r0:r0 + cc, :] = vnb
                    o = (jnp.dot(p["qg"][r0:r0 + cc, :], sb, preferred_element_type=F32)
                         + jnp.dot(p["attn"][r0:r0 + cc, :], vnew_ref[...], preferred_element_type=F32))
                    out_ref = of_ref if d == 0 else ob_ref
                    out_ref[pl.ds(pl.multiple_of(blk[d][u] * c + r0, cc), cc), hh * dk:(hh + 1) * dk] = o
                    upd = lax.dot_general(p["kd"][r0:r0 + cc, :], vnb, tn, preferred_element_type=F32)
                    s_ref[...] = s * p["etot"][r0:r0 + 1, :] + upd
        return carry

    lax.fori_loop(0, n // unroll, scan, 0)

    for hh in range(hps):
        for d in range(2):
            sfin_ref[0, d, hh] = s_refs[2 * hh + d][...]

    def fin(i, carry):
        rows = pl.ds(pl.multiple_of(i * c, c), c)
        for hh in range(hps):
            ln = slice(hh * dk, (hh + 1) * dk)
            o = of_ref[rows, ln] + ob_ref[rows, ln]
            o = o * lax.rsqrt(jnp.mean(o * o, axis=-1, keepdims=True) + EPS) * ng_ref[...]
            o_ref[rows, ln] = (o * _silu(z_ref[rows, ln].astype(F32))).astype(o_ref.dtype)
        return carry

    lax.fori_loop(0, n, fin, 0)


def deltanet_core(qkv, z, gates, w_conv_t, a_row, dt_row, norm_g, s0, *, row0, nseq, seqlen, seg, nheads, hps,
                  unroll=1):
    m, d3 = qkv.shape
    d = d3 // 3
    dk = d // nheads
    assert dk == 128 and row0 % seqlen == 0 and nheads % hps == 0
    chunk = min(seqlen, 256)
    assert seqlen % (chunk * unroll) == 0 and chunk % seg == 0 and chunk % DN_CHUNK == 0
    rb = row0 // seqlen
    nhb = nheads // hps
    kw = w_conv_t.shape[0]
    g4 = gates.shape[1]
    wd = hps * dk

    def colspec(part):
        return pl.BlockSpec((seqlen, wd), lambda b, h: (rb + b, part * nhb + h))

    def wspec(part):
        return pl.BlockSpec((kw, wd), lambda b, h: (0, part * nhb + h))

    kern = functools.partial(_dn_kernel, seg=seg, chunk=chunk, nheads=nheads, hps=hps, unroll=unroll)
    st_spec = pl.BlockSpec((1, 2, hps, dk, dk), lambda b, h: (b, 0, h, 0, 0))
    out, sfin = pl.pallas_call(
        kern,
        grid=(nseq, nhb),
        in_specs=[colspec(0), colspec(1), colspec(2),
                  pl.BlockSpec((seqlen, wd), lambda b, h: (rb + b, h)),
                  pl.BlockSpec((seqlen, g4), lambda b, h: (rb + b, 0)),
                  wspec(0), wspec(1), wspec(2),
                  pl.BlockSpec((1, g4), lambda b, h: (0, 0)),
                  pl.BlockSpec((1, g4), lambda b, h: (0, 0)),
                  pl.BlockSpec((1, dk), lambda b, h: (0, 0)),
                  st_spec],
        out_specs=[pl.BlockSpec((seqlen, wd), lambda b, h: (b, h)), st_spec],
        out_shape=[jax.ShapeDtypeStruct((nseq * seqlen, d), BF16),
                   jax.ShapeDtypeStruct((nseq, 2, nheads, dk, dk), F32)],
        scratch_shapes=([pltpu.VMEM((seqlen, wd), F32) for _ in range(5)]
                        + [pltpu.VMEM((dk, dk), F32) for _ in range(2 * hps)]
                        + [pltpu.VMEM((chunk, dk), BF16) for _ in range(2 * hps * unroll)]),
        compiler_params=_cparams(("parallel", "parallel")),
        name="deltanet_core",
    )(qkv, qkv, qkv, z, gates, w_conv_t, w_conv_t, w_conv_t, a_row, dt_row, norm_g.reshape(1, dk), s0)
    return out, sfin


def _dot3(a_hi, a_lo, x):
    x_hi, x_lo = _split_bf16(x)
    acc = jnp.dot(a_hi, x_hi, preferred_element_type=F32)
    acc = acc + jnp.dot(a_hi, x_lo, preferred_element_type=F32)
    return acc + jnp.dot(a_lo, x_hi, preferred_element_type=F32)


def _dft_mats(p):
    n = 2 * p
    k = np.arange(p, dtype=np.float64)[:, None]
    s = np.arange(n, dtype=np.float64)[None, :]
    th = np.pi * (2.0 * k + 1.0) * s / n
    fwd = np.concatenate([np.cos(th), -np.sin(th)], axis=0)
    tht = th[:, :p].T
    inv = np.concatenate([np.cos(tht), -np.sin(tht)], axis=1) * (2.0 / n)
    return fwd, inv


def _hi_lo_const(a):
    a32 = np.asarray(a, np.float32)
    hi = a32.astype(BF16)
    lo = (a32 - hi.astype(np.float32)).astype(BF16)
    return jnp.asarray(hi), jnp.asarray(lo)


def _spec_kernel(ah_ref, al_ref, t_ref, o_ref):
    o_ref[0] = _dot3(ah_ref[...], al_ref[...], t_ref[0])


def filter_spectra(taps, p):
    nd, n, c = taps.shape
    fwd, _ = _dft_mats(p)
    fh, fl = _hi_lo_const(fwd)
    tn = _pick(c, (512, 256, 128))
    return pl.pallas_call(
        _spec_kernel,
        grid=(nd, c // tn),
        in_specs=[pl.BlockSpec((n, n), lambda d, j: (0, 0)),
                  pl.BlockSpec((n, n), lambda d, j: (0, 0)),
                  pl.BlockSpec((1, n, tn), lambda d, j: (d, 0, j))],
        out_specs=pl.BlockSpec((1, n, tn), lambda d, j: (d, 0, j)),
        out_shape=jax.ShapeDtypeStruct((nd, n, c), F32),
        compiler_params=_cparams(("parallel", "parallel")),
        name="hyena_filter_spectra",
    )(fh, fl, taps)


def _hy_kernel(x0_ref, x1_ref, v_ref, w0_ref, w1_ref, w2_ref, bias_ref, spec_ref,
               fh_ref, ih_ref, o_ref, vv_ref, vh_ref, acc_ref, *, seg, p, band):
    l, tc = v_ref.shape
    nb = l // p

    def fwd(j, carry):
        rows = pl.ds(pl.multiple_of(j * p, p), p)
        vv = _seg_conv(v_ref[rows, :], w2_ref, seg) * _seg_conv(x1_ref[rows, :], w1_ref, seg)
        vv_ref[rows, :] = vv
        vh_ref[j] = jnp.dot(fh_ref[...], vv.astype(BF16), preferred_element_type=F32)
        return carry

    lax.fori_loop(0, nb, fwd, 0)

    rt = 64

    def out_block(i, carry):
        def row_tile(r, c2):
            rr = pl.ds(pl.multiple_of(r * rt, rt), rt)
            ri = pl.ds(pl.multiple_of(p + r * rt, rt), rt)

            j0 = (i // band) * band

            def mac(jj, acc):
                are, aim = acc
                j = j0 + jj
                d = i - j + (band - 1)
                fre = spec_ref[d, rr, :]
                fim = spec_ref[d, ri, :]
                xre = vh_ref[j, rr, :]
                xim = vh_ref[j, ri, :]
                return (are + fre * xre - fim * xim, aim + fre * xim + fim * xre)

            zero = jnp.zeros((rt, tc), F32)
            are, aim = lax.fori_loop(0, band, mac, (zero, zero))
            acc_ref[rr, :] = are
            acc_ref[ri, :] = aim
            return c2

        lax.fori_loop(0, p // rt, row_tile, 0)
        y = jnp.dot(ih_ref[...], acc_ref[...].astype(BF16), preferred_element_type=F32)
        rows = pl.ds(pl.multiple_of(i * p, p), p)
        y = y + vv_ref[rows, :] * bias_ref[...]
        o_ref[rows, :] = (y * _seg_conv(x0_ref[rows, :], w0_ref, seg)).astype(o_ref.dtype)
        return carry

    lax.fori_loop(0, nb, out_block, 0)


def hyena_core(u, w_conv_t, bias, spec, *, row0, nseq, seqlen, seg, p, spb=1):
    m, c3 = u.shape
    c = c3 // 3
    tc = 128
    assert c % tc == 0 and seqlen % p == 0 and p % seg == 0 and nseq % spb == 0
    band = seqlen // p
    assert spec.shape[0] == 2 * band - 1
    nseq, seqlen = nseq // spb, seqlen * spb
    assert row0 % seqlen == 0
    nct = c // tc
    rb = row0 // seqlen
    nd = spec.shape[0]
    kw = w_conv_t.shape[0]
    fwd, inv = _dft_mats(p)
    fh, _ = _hi_lo_const(fwd[:, :p])
    ih, _ = _hi_lo_const(inv)

    def colspec(part):
        return pl.BlockSpec((seqlen, tc), lambda j, b: (rb + b, part * nct + j))

    def wspec(part):
        return pl.BlockSpec((kw, tc), lambda j, b: (0, part * nct + j))

    const = lambda shape: pl.BlockSpec(shape, lambda j, b: (0,) * len(shape))
    kern = functools.partial(_hy_kernel, seg=seg, p=p, band=band)
    return pl.pallas_call(
        kern,
        grid=(nct, nseq),
        in_specs=[colspec(0), colspec(1), colspec(2), wspec(0), wspec(1), wspec(2),
                  pl.BlockSpec((1, tc), lambda j, b: (0, j)),
                  pl.BlockSpec((nd, 2 * p, tc), lambda j, b: (0, 0, j)),
                  const((2 * p, p)), const((p, 2 * p))],
        out_specs=pl.BlockSpec((seqlen, tc), lambda j, b: (b, j)),
        out_shape=jax.ShapeDtypeStruct((nseq * seqlen, c), BF16),
        scratch_shapes=[pltpu.VMEM((seqlen, tc), F32),
                        pltpu.VMEM((seqlen // p, 2 * p, tc), F32),
                        pltpu.VMEM((2 * p, tc), F32)],
        compiler_params=_cparams(("parallel", "parallel")),
        name="hyena_core",
    )(u, u, u, w_conv_t, w_conv_t, w_conv_t, bias.reshape(1, c), spec, fh, ih)


def _filter_kernel(feat_ref, w1_ref, b1_ref, w2_ref, b2_ref, f_ref, w3_ref, dec_ref, o_ref):
    feat = feat_ref[...]
    hid = jnp.sin(f_ref[0:1, :] * (jnp.dot(feat, w1_ref[...], precision=HI, preferred_element_type=F32)
                                  + b1_ref[...]))
    hid = jnp.sin(f_ref[1:2, :] * (jnp.dot(hid, w2_ref[...], precision=HI, preferred_element_type=F32)
                                  + b2_ref[...]))
    h = jnp.dot(hid, w3_ref[...], precision=HI, preferred_element_type=F32)
    o_ref[...] = h * jnp.exp(-feat[:, 0:1] * jnp.abs(dec_ref[...]))


def hyena_filter_taps(l, p, w1, b1, w2, b2, w3, freq, decay):
    t = np.linspace(0.0, 1.0, l, dtype=np.float32)[:, None]
    bands = np.linspace(1e-4, HY_BANDS - 1, HY_BANDS, dtype=np.float32)
    ang = (np.float32(2.0 * math.pi / l) * np.arange(l, dtype=np.float32)[:, None] * bands).astype(np.float32)
    feat = np.concatenate([t, np.cos(ang), -np.sin(ang)], axis=-1).astype(np.float32)
    kf = 128
    nfeat, hidden = w1.shape
    c2 = w3.shape[1]
    feat = jnp.asarray(np.pad(feat, ((0, 0), (0, kf - nfeat))))
    w1p = jnp.pad(w1, ((0, kf - nfeat), (0, 0)))
    tcn = _pick(c2, (1024, 512, 256, 128))
    const = lambda shape: pl.BlockSpec(shape, lambda j: (0,) * len(shape))
    h = pl.pallas_call(
        _filter_kernel,
        grid=(c2 // tcn,),
        in_specs=[const((l, kf)), const((kf, hidden)), const((1, hidden)), const((hidden, hidden)),
                  const((1, hidden)), const((2, hidden)),
                  pl.BlockSpec((hidden, tcn), lambda j: (0, j)),
                  pl.BlockSpec((1, tcn), lambda j: (0, j))],
        out_specs=pl.BlockSpec((l, tcn), lambda j: (0, j)),
        out_shape=jax.ShapeDtypeStruct((l, c2), F32),
        compiler_params=_cparams(("parallel",)),
        name="hyena_filter",
    )(feat, w1p, b1[None, :], w2, b2[None, :], freq, w3, decay.reshape(1, c2))
    h_fwd, h_bwd = h[:, :c2 // 2], h[:, c2 // 2:]
    denom = (jnp.sum(jnp.abs(h_fwd), axis=0, keepdims=True)
             + jnp.sum(jnp.abs(h_bwd[1:]), axis=0, keepdims=True) + EPS)
    hcat = jnp.concatenate([h_fwd, h_bwd], axis=0) / denom
    nb = l // p
    n = 2 * p
    mi = np.arange(n)
    delta = np.where(mi < p, mi, mi - n)
    sign = np.where(mi < p, 1.0, -1.0) * (mi != p)
    off = np.arange(-(nb - 1), nb)[:, None] * p + delta[None, :]
    valid = np.abs(off) < l
    idx = np.where(valid, np.where(off >= 0, off, l - off), 0).astype(np.int32)
    coef = (valid * sign[None, :]).astype(np.float32)
    return hcat[jnp.asarray(idx)] * jnp.asarray(coef)[:, :, None]


def _gather_kernel(idx_ref, src_ref, o_ref, stage_ref, sem):
    i = pl.program_id(0)
    tg = stage_ref.shape[0]

    def issue(r, carry):
        row = idx_ref[i * tg + r]
        pltpu.make_async_copy(src_ref.at[pl.ds(row, 1)], stage_ref.at[pl.ds(r, 1)], sem).start()
        return carry

    lax.fori_loop(0, tg, issue, 0)

    def drain(r, carry):
        pltpu.make_async_copy(src_ref.at[pl.ds(0, 1)], stage_ref.at[pl.ds(r, 1)], sem).wait()
        return carry

    lax.fori_loop(0, tg, drain, 0)
    o_ref[...] = stage_ref[...].astype(o_ref.dtype)


def gather_rows(src, idx, tg=512):
    npick = idx.shape[0]
    d = src.shape[1]
    assert npick % tg == 0
    return pl.pallas_call(
        _gather_kernel,
        grid_spec=pltpu.PrefetchScalarGridSpec(
            num_scalar_prefetch=1,
            grid=(npick // tg,),
            in_specs=[pl.BlockSpec(memory_space=pl.ANY)],
            out_specs=pl.BlockSpec((tg, d), lambda i, idx_ref: (i, 0)),
            scratch_shapes=[pltpu.VMEM((tg, d), F32), pltpu.SemaphoreType.DMA(())],
        ),
        out_shape=jax.ShapeDtypeStruct((npick, d), BF16),
        compiler_params=_cparams(("arbitrary",)),
        name="moe_gather",
    )(idx, src)


def _ffn_kernel(xs_ref, wg_ref, wu_ref, wd_ref, o_ref, hid_ref, *, nf, tf):
    j = pl.program_id(1)

    @pl.when(j < nf)
    def _():
        xs = xs_ref[...]
        g = jnp.dot(xs, wg_ref[0].astype(BF16), preferred_element_type=F32)
        u = jnp.dot(xs, wu_ref[0].astype(BF16), preferred_element_type=F32)
        hid_ref[j] = (_silu(g) * u).astype(BF16)

    @pl.when(j >= nf)
    def _():
        acc = jnp.dot(hid_ref[0], wd_ref[0, 0:tf, :].astype(BF16), preferred_element_type=F32)
        for f in range(1, nf):
            acc = acc + jnp.dot(hid_ref[f], wd_ref[0, f * tf:(f + 1) * tf, :].astype(BF16),
                                preferred_element_type=F32)
        o_ref[...] = acc


def expert_ffn(xs, w_gate, w_up, w_down, lyr, cap):
    npick, d = xs.shape
    _, e, _, ff = w_gate.shape
    nse = npick // cap
    tf = _pick(ff, (256, 128))
    td = _pick(d, (512, 256, 128))
    nf, ndt = ff // tf, d // td
    kern = functools.partial(_ffn_kernel, nf=nf, tf=tf)
    return pl.pallas_call(
        kern,
        grid=(nse, nf + ndt),
        in_specs=[pl.BlockSpec((cap, d), lambda s, j: (s, 0)),
                  pl.BlockSpec((None, 1, d, tf), lambda s, j: (lyr, s % e, 0, jnp.minimum(j, nf - 1))),
                  pl.BlockSpec((None, 1, d, tf), lambda s, j: (lyr, s % e, 0, jnp.minimum(j, nf - 1))),
                  pl.BlockSpec((None, 1, ff, td), lambda s, j: (lyr, s % e, 0, jnp.maximum(j - nf, 0)))],
        out_specs=pl.BlockSpec((cap, td), lambda s, j: (s, jnp.maximum(j - nf, 0))),
        out_shape=jax.ShapeDtypeStruct((npick, d), F32),
        scratch_shapes=[pltpu.VMEM((nf, cap, tf), BF16)],
        compiler_params=_cparams(("parallel", "arbitrary")),
        name="moe_expert_ffn",
    )(xs, w_gate, w_up, w_down)


def _scatter_kernel(idx_ref, ye_ref, coef_ref, sel_ref, ga_ref, gb_ref, x_in_ref, x_ref,
                    stage_ref, sem_r, sem_w):
    del x_in_ref
    i = pl.program_id(0)
    tg = stage_ref.shape[0]

    def rd(r, carry):
        row = idx_ref[i * tg + r]
        pltpu.make_async_copy(x_ref.at[pl.ds(row, 1)], stage_ref.at[pl.ds(r, 1)], sem_r).start()
        return carry

    lax.fori_loop(0, tg, rd, 0)

    def rd_wait(r, carry):
        pltpu.make_async_copy(x_ref.at[pl.ds(0, 1)], stage_ref.at[pl.ds(r, 1)], sem_r).wait()
        return carry

    lax.fori_loop(0, tg, rd_wait, 0)
    gvec = jnp.where(sel_ref[...] > 0.5, gb_ref[0], ga_ref[0])
    stage_ref[...] = stage_ref[...] + gvec * (coef_ref[...] * ye_ref[...])

    def wr(r, carry):
        row = idx_ref[i * tg + r]
        pltpu.make_async_copy(stage_ref.at[pl.ds(r, 1)], x_ref.at[pl.ds(row, 1)], sem_w).start()
        return carry

    lax.fori_loop(0, tg, wr, 0)

    def wr_wait(r, carry):
        pltpu.make_async_copy(stage_ref.at[pl.ds(r, 1)], x_ref.at[pl.ds(0, 1)], sem_w).wait()
        return carry

    lax.fori_loop(0, tg, wr_wait, 0)


def scatter_add_rows(x, ye, idx, coef, sel, gab, picks_per_stream, tg=512):
    npick, d = ye.shape
    assert npick % tg == 0 and picks_per_stream % tg == 0
    cps = picks_per_stream // tg
    return pl.pallas_call(
        _scatter_kernel,
        grid_spec=pltpu.PrefetchScalarGridSpec(
            num_scalar_prefetch=1,
            grid=(npick // tg,),
            in_specs=[pl.BlockSpec((tg, d), lambda i, idx_ref: (i, 0)),
                      pl.BlockSpec((tg, 1), lambda i, idx_ref: (i, 0)),
                      pl.BlockSpec((tg, 1), lambda i, idx_ref: (i, 0)),
                      pl.BlockSpec((1, 1, d), lambda i, idx_ref: (2 * (i // cps), 0, 0)),
                      pl.BlockSpec((1, 1, d), lambda i, idx_ref: (2 * (i // cps) + 1, 0, 0)),
                      pl.BlockSpec(memory_space=pl.ANY)],
            out_specs=pl.BlockSpec(memory_space=pl.ANY),
            scratch_shapes=[pltpu.VMEM((tg, d), F32), pltpu.SemaphoreType.DMA(()),
                            pltpu.SemaphoreType.DMA(())],
        ),
        out_shape=jax.ShapeDtypeStruct(x.shape, x.dtype),
        input_output_aliases={6: 0},
        compiler_params=_cparams(("arbitrary",)),
        name="moe_scatter_add",
    )(idx, ye, coef, sel, gab, gab, x)


def moe_layer(x, hn, aff, g2, w_gate, w_up, w_down, lyr, n_ctx, dec_seq):
    m, d = x.shape
    e = aff.shape[1]
    n_dec = m - n_ctx
    assert n_ctx == n_dec, "merged MoE kernels need equal routed-set sizes"
    cap = EC_FACTOR * n_ctx // e
    gate0, idx0 = lax.top_k(aff[:n_ctx].T, cap)
    gate1, idx1 = lax.top_k(aff[n_ctx:].T, cap)
    idx = jnp.concatenate([idx0.reshape(-1), idx1.reshape(-1) + n_ctx]).astype(jnp.int32)
    coef = jnp.concatenate([gate0.reshape(-1), gate1.reshape(-1)])[:, None]
    sel = (idx >= n_ctx + dec_seq).astype(F32)[:, None]
    ngrp = g2.shape[0]
    gab = jnp.stack([g2[0], g2[0], g2[1], g2[min(2, ngrp - 1)]])[:, None, :]
    xs = gather_rows(hn, idx)
    ye = expert_ffn(xs, w_gate, w_up, w_down, lyr, cap)
    return scatter_add_rows(x, ye, idx, coef, sel, gab, e * cap)


def kernel(x_prompt, x_sample, state_dn, c, c_ctx, ada_w, ada_b, norm1_g, norm2_g, dn_w_in, dn_conv, dn_a_log, dn_dt_bias, dn_norm_g, dn_w_out, hy_w_in, hy_conv, hy_w1, hy_b1, hy_w2, hy_b2, hy_w3, hy_freq, hy_decay, hy_bias, hy_w_out, moe_router, moe_w_gate, moe_w_up, moe_w_down, final_g):
    bsz, seq, d = x_prompt.shape
    dbsz, dseq, _ = x_sample.shape
    assert dbsz == 2, "scatter group select handles two latent requests"
    depth = ada_w.shape[0]
    nheads = dn_a_log.shape[2]
    n_ctx = bsz * seq
    n_dec = dbsz * dseq
    x = jnp.concatenate([x_prompt.reshape(n_ctx, d), x_sample.reshape(n_dec, d)], axis=0)

    cvec = jnp.concatenate([c_ctx[None, :], c], axis=0)
    ngrp = cvec.shape[0]
    sc_in = jnp.zeros((16, d), F32).at[:ngrp].set(cvec)
    sc_in = (sc_in * (1.0 / (1.0 + jnp.exp(-sc_in)))).astype(BF16)

    p_ctx = min(seq, 512)
    p_dec = min(dseq, 512)
    states = []
    for i in range(depth):
        mod = matmul(sc_in, ada_w, widx=i, bias=ada_b[i], tm=16)[:ngrp]
        sh1, sc1, g1, sh2, sc2, g2 = [mod[:, t * d:(t + 1) * d] for t in range(N_MOD)]
        hn = norm_mod(x, norm1_g[i], sc1, sh1, n_ctx, dseq, BF16)
        j = i // 2
        if i % 2 == 0:
            qkv = matmul(hn, dn_w_in, widx=j, col0=0, ncols=3 * d)
            z = matmul(hn, dn_w_in, widx=j, col0=3 * d, ncols=d, out_dtype=BF16)
            gates = matmul(hn, dn_w_in, widx=j, col0=4 * d, ncols=4 * nheads, tn=4 * nheads)
            wct = dn_conv[j].T
            a_row = jnp.concatenate([jnp.zeros((2 * nheads,), F32), dn_a_log[j].reshape(-1)])[None, :]
            dt_row = jnp.concatenate([jnp.zeros((2 * nheads,), F32), dn_dt_bias[j].reshape(-1)])[None, :]
            dk = d // nheads
            o_ctx, s_ctx = deltanet_core(qkv, z, gates, wct, a_row, dt_row, dn_norm_g[j],
                                         jnp.zeros((bsz, 2, nheads, dk, dk), F32),
                                         row0=0, nseq=bsz, seqlen=seq, seg=seq, nheads=nheads, hps=4)
            o_dec, _ = deltanet_core(qkv, z, gates, wct, a_row, dt_row, dn_norm_g[j], state_dn[:, j],
                                     row0=n_ctx, nseq=dbsz, seqlen=dseq, seg=GRID_W, nheads=nheads, hps=1,
                                     unroll=2)
            states.append(s_ctx)
            x = matmul(o_ctx, dn_w_out, a2=o_dec, widx=j, resid=(x, g1, n_ctx, dseq))
        else:
            u = matmul(hn, hy_w_in, widx=j)
            wct = hy_conv[j].T
            filt = (hy_w1[j], hy_b1[j], hy_w2[j], hy_b2[j], hy_w3[j], hy_freq[j], hy_decay[j])
            spec_ctx = filter_spectra(hyena_filter_taps(seq, p_ctx, *filt), p_ctx)
            spec_dec = filter_spectra(hyena_filter_taps(dseq, p_dec, *filt), p_dec)
            y_ctx = hyena_core(u, wct, hy_bias[j], spec_ctx, row0=0, nseq=bsz, seqlen=seq, seg=seq, p=p_ctx,
                               spb=_pick(bsz, (8, 4, 2)))
            y_dec = hyena_core(u, wct, hy_bias[j], spec_dec, row0=n_ctx, nseq=dbsz, seqlen=dseq,
                               seg=GRID_W, p=p_dec)
            x = matmul(y_ctx, hy_w_out, a2=y_dec, widx=j, resid=(x, g1, n_ctx, dseq))
        hn2, aff = norm_mod(x, norm2_g[i], sc2, sh2, n_ctx, dseq, F32, w_router=moe_router[i])
        x = moe_layer(x, hn2, aff, g2, moe_w_gate, moe_w_up, moe_w_down, i, n_ctx, dseq)
    y = final_norm(x, final_g)
    y_prompt = y[:n_ctx].reshape(bsz, seq, d)
    y_sample = y[n_ctx:].reshape(dbsz, dseq, d)
    return (y_prompt, y_sample, jnp.stack(states, axis=1))
```

```python
import functools
import math

import jax
import jax.numpy as jnp
import numpy as np
from jax import lax
from jax.experimental import pallas as pl
from jax.experimental.pallas import tpu as pltpu

EPS = 1e-6
N_MOD = 6
GRID_W = 64
DN_CHUNK = 64
EC_FACTOR = 2
HY_BANDS = 16
V7X_VMEM_LIMIT = 56 * 1024 * 1024

F32 = jnp.float32
BF16 = jnp.bfloat16
HI = lax.Precision.HIGHEST


def _cparams(sem):
    return pltpu.CompilerParams(dimension_semantics=sem, vmem_limit_bytes=V7X_VMEM_LIMIT)


def _group_of_row(r0, n_ctx, dec_seq):
    return jnp.where(r0 < n_ctx, 0, 1 + (r0 - n_ctx) // dec_seq)


def _pick(n, cands):
    for c in cands:
        if n % c == 0:
            return c
    return n


def _norm_mod_kernel(x_ref, g_ref, sc_ref, sh_ref, o_ref):
    x = x_ref[...]
    y = x * lax.rsqrt(jnp.mean(x * x, axis=-1, keepdims=True) + EPS)
    y = y * g_ref[...]
    o_ref[...] = (y * (1.0 + sc_ref[0]) + sh_ref[0]).astype(o_ref.dtype)


def _norm_mod_router_kernel(x_ref, g_ref, sc_ref, sh_ref, wr_ref, o_ref, aff_ref):
    x = x_ref[...]
    y = x * lax.rsqrt(jnp.mean(x * x, axis=-1, keepdims=True) + EPS)
    y = y * g_ref[...]
    hn = y * (1.0 + sc_ref[0]) + sh_ref[0]
    o_ref[...] = hn.astype(o_ref.dtype)
    logits = jnp.dot(hn, wr_ref[...], precision=HI, preferred_element_type=F32)
    m = jnp.max(logits, axis=-1, keepdims=True)
    e = jnp.exp(logits - m)
    aff_ref[...] = e / jnp.sum(e, axis=-1, keepdims=True)


def norm_mod(x, g, sc, sh, n_ctx, dec_seq, out_dtype, w_router=None):
    m, d = x.shape
    tm = _pick(m, (256, 128, 64, 32, 16, 8))
    assert n_ctx % tm == 0 and dec_seq % tm == 0
    grp = lambda i: (_group_of_row(i * tm, n_ctx, dec_seq), 0, 0)
    in_specs = [
        pl.BlockSpec((tm, d), lambda i: (i, 0)),
        pl.BlockSpec((1, d), lambda i: (0, 0)),
        pl.BlockSpec((1, 1, d), grp),
        pl.BlockSpec((1, 1, d), grp),
    ]
    args = [x, g.reshape(1, d), sc[:, None, :], sh[:, None, :]]
    if w_router is None:
        return pl.pallas_call(
            _norm_mod_kernel,
            grid=(m // tm,),
            in_specs=in_specs,
            out_specs=pl.BlockSpec((tm, d), lambda i: (i, 0)),
            out_shape=jax.ShapeDtypeStruct((m, d), out_dtype),
            compiler_params=_cparams(("parallel",)),
            name="norm_mod",
        )(*args)
    e = w_router.shape[1]
    return pl.pallas_call(
        _norm_mod_router_kernel,
        grid=(m // tm,),
        in_specs=in_specs + [pl.BlockSpec((d, e), lambda i: (0, 0))],
        out_specs=[pl.BlockSpec((tm, d), lambda i: (i, 0)),
                   pl.BlockSpec((tm, e), lambda i: (i, 0))],
        out_shape=[jax.ShapeDtypeStruct((m, d), out_dtype),
                   jax.ShapeDtypeStruct((m, e), F32)],
        compiler_params=_cparams(("parallel",)),
        name="norm_mod_router",
    )(*args, w_router)


def _final_norm_kernel(x_ref, g_ref, o_ref):
    x = x_ref[...]
    y = x * lax.rsqrt(jnp.mean(x * x, axis=-1, keepdims=True) + EPS)
    o_ref[...] = y * g_ref[...]


def final_norm(x, g):
    m, d = x.shape
    tm = _pick(m, (256, 128, 64, 32, 16, 8))
    return pl.pallas_call(
        _final_norm_kernel,
        grid=(m // tm,),
        in_specs=[pl.BlockSpec((tm, d), lambda i: (i, 0)),
                  pl.BlockSpec((1, d), lambda i: (0, 0))],
        out_specs=pl.BlockSpec((tm, d), lambda i: (i, 0)),
        out_shape=jax.ShapeDtypeStruct((m, d), F32),
        compiler_params=_cparams(("parallel",)),
        name="final_norm",
    )(x, g.reshape(1, d))


def _mm_kernel(a_ref, w_ref, o_ref):
    acc = jnp.dot(a_ref[...], w_ref[...].astype(BF16), preferred_element_type=F32)
    o_ref[...] = acc.astype(o_ref.dtype)


def _mm_bias_kernel(a_ref, w_ref, b_ref, o_ref):
    acc = jnp.dot(a_ref[...], w_ref[...].astype(BF16), preferred_element_type=F32)
    o_ref[...] = (acc + b_ref[...]).astype(o_ref.dtype)


def _mm_resid_kernel(a1_ref, a2_ref, w_ref, x_ref, gate_ref, o_ref, *, nb1):
    i = pl.program_id(0)
    w = w_ref[...].astype(BF16)

    @pl.when(i < nb1)
    def _():
        o_ref[...] = x_ref[...] + gate_ref[0] * jnp.dot(a1_ref[...], w, preferred_element_type=F32)

    @pl.when(i >= nb1)
    def _():
        o_ref[...] = x_ref[...] + gate_ref[0] * jnp.dot(a2_ref[...], w, preferred_element_type=F32)


def matmul(a, w, *, a2=None, widx=None, col0=0, ncols=None, out_dtype=F32, bias=None, resid=None,
           tm=None, tn=None):
    m1, k = a.shape
    m = m1 + (0 if a2 is None else a2.shape[0])
    n = w.shape[-1] - col0 if ncols is None else ncols
    tm = tm or _pick(m, (1024, 512, 256, 128, 64, 32, 16, 8))
    tn = tn or _pick(n, (256, 128) if resid is not None else (512, 256, 128))
    assert col0 % tn == 0 and m % tm == 0 and n % tn == 0
    cb = col0 // tn
    grid = (m // tm, n // tn)
    a_spec = pl.BlockSpec((tm, k), lambda i, j: (i, 0))
    if widx is None:
        w_spec = pl.BlockSpec((k, tn), lambda i, j: (0, cb + j))
    else:
        w_spec = pl.BlockSpec((None, k, tn), lambda i, j: (widx, 0, cb + j))
    o_spec = pl.BlockSpec((tm, tn), lambda i, j: (i, j))
    out_shape = jax.ShapeDtypeStruct((m, n), out_dtype)
    cp = _cparams(("parallel", "parallel"))
    if resid is not None:
        x, gate, n_ctx, dec_seq = resid
        assert n_ctx % tm == 0 and dec_seq % tm == 0 and a2 is not None and m1 % tm == 0
        nb1 = m1 // tm
        g_spec = pl.BlockSpec((1, 1, tn), lambda i, j: (_group_of_row(i * tm, n_ctx, dec_seq), 0, j))
        a1_spec = pl.BlockSpec((tm, k), lambda i, j: (jnp.minimum(i, nb1 - 1), 0))
        a2_spec = pl.BlockSpec((tm, k), lambda i, j: (jnp.maximum(i - nb1, 0), 0))
        return pl.pallas_call(
            functools.partial(_mm_resid_kernel, nb1=nb1), grid=grid,
            in_specs=[a1_spec, a2_spec, w_spec, o_spec, g_spec], out_specs=o_spec,
            out_shape=out_shape, compiler_params=cp, name="matmul_resid",
        )(a, a2, w, x, gate[:, None, :])
    if bias is not None:
        b_spec = pl.BlockSpec((1, tn), lambda i, j: (0, j))
        return pl.pallas_call(
            _mm_bias_kernel, grid=grid,
            in_specs=[a_spec, w_spec, b_spec], out_specs=o_spec,
            out_shape=out_shape, compiler_params=cp, name="matmul_bias",
        )(a, w, bias.reshape(1, n))
    return pl.pallas_call(
        _mm_kernel, grid=grid,
        in_specs=[a_spec, w_spec], out_specs=o_spec,
        out_shape=out_shape, compiler_params=cp, name="matmul",
    )(a, w)


def _silu(x):
    return x * (1.0 / (1.0 + jnp.exp(-x)))


def _softplus(x):
    return jnp.maximum(x, 0.0) + jnp.log1p(jnp.exp(-jnp.abs(x)))


def _split_bf16(x):
    hi = x.astype(BF16)
    lo = (x - hi.astype(F32)).astype(BF16)
    return hi, lo


def _seg_conv(x, w_ref, seg):
    r, _ = x.shape
    k = w_ref.shape[0]
    p = k // 2
    assert seg & (seg - 1) == 0
    pos = jnp.bitwise_and(lax.broadcasted_iota(jnp.int32, x.shape, 0), seg - 1)
    y = x * w_ref[p:p + 1, :]
    for j in range(k):
        off = j - p
        if off == 0:
            continue
        shifted = pltpu.roll(x, (-off) % r, axis=0)
        ok = jnp.logical_and(pos + off >= 0, pos + off < seg)
        y = y + jnp.where(ok, shifted, 0.0) * w_ref[j:j + 1, :]
    return y


def _pdot(a, b):
    acc = None
    for ia, at in enumerate(a):
        for ib, bt in enumerate(b):
            if ia + ib > 1:
                continue
            t = jnp.dot(at, bt, preferred_element_type=F32)
            acc = t if acc is None else acc + t
    return acc


def _dn_kernel(q_ref, k_ref, v_ref, z_ref, gt_ref, wq_ref, wk_ref, wv_ref, arow_ref, dtrow_ref,
               ng_ref, s0_ref, o_ref, sfin_ref,
               qn_ref, kn_ref, vn_ref, of_ref, ob_ref, *rest, seg, chunk, nheads, hps, unroll):
    s_refs = rest[:2 * hps]
    vnew_refs = rest[2 * hps:]
    l = q_ref.shape[0]
    dk = q_ref.shape[1] // hps
    c = chunk
    n = l // c
    hb = pl.program_id(1)

    def prep(i, carry):
        rows = pl.ds(pl.multiple_of(i * c, c), c)
        for hh in range(hps):
            ln = slice(hh * dk, (hh + 1) * dk)
            q = _silu(_seg_conv(q_ref[rows, ln], wq_ref.at[:, ln], seg))
            k = _silu(_seg_conv(k_ref[rows, ln], wk_ref.at[:, ln], seg))
            v = _silu(_seg_conv(v_ref[rows, ln], wv_ref.at[:, ln], seg))
            q = q * lax.rsqrt(jnp.sum(q * q, axis=-1, keepdims=True) + EPS) * (dk ** -0.5)
            k = k * lax.rsqrt(jnp.sum(k * k, axis=-1, keepdims=True) + EPS)
            qn_ref[rows, ln] = q
            kn_ref[rows, ln] = k
            vn_ref[rows, ln] = v
        return carry

    lax.fori_loop(0, n, prep, 0)

    for hh in range(hps):
        for d in range(2):
            s_refs[2 * hh + d][...] = s0_ref[0, d, hh]

    for r in vnew_refs:
        r[...] = jnp.zeros_like(r)

    cc = DN_CHUNK
    sh = int(math.log2(cc))
    nsub = c // cc
    ri = lax.broadcasted_iota(jnp.int32, (c, c), 0)
    ci = lax.broadcasted_iota(jnp.int32, (c, c), 1)
    same_lv = {lv: jnp.right_shift(ri, lv) == jnp.right_shift(ci, lv) for lv in range(3, sh + 1)}
    same = same_lv[sh]
    same8 = same_lv[3]
    eye = (ri == ci).astype(F32)
    incl_d = (jnp.logical_and(same, ri >= ci), jnp.logical_and(same, ri <= ci))
    strict_d = (jnp.logical_and(same, ri > ci), jnp.logical_and(same, ri < ci))
    incl_b = tuple(m.astype(BF16) for m in incl_d)
    same_b = same.astype(BF16)
    lane = lax.broadcasted_iota(jnp.int32, (c, gt_ref.shape[1]), 1)
    nt = (((1,), (1,)), ((), ()))
    tn = (((0,), (0,)), ((), ()))
    probs = [(hh, d, u) for u in range(unroll) for hh in range(hps) for d in range(2)]

    def col(tile, idx):
        return jnp.sum(jnp.where(lane == idx, tile, 0.0), axis=-1, keepdims=True)

    def split3(x):
        t1 = x.astype(BF16)
        r = x - t1.astype(F32)
        t2 = r.astype(BF16)
        t3 = (r - t2.astype(F32)).astype(BF16)
        return t1, t2, t3

    def mask_sum(mask_b, terms):
        return (jnp.dot(mask_b, terms[0], preferred_element_type=F32)
                + jnp.dot(mask_b, terms[1], preferred_element_type=F32)
                + jnp.dot(mask_b, terms[2], preferred_element_type=F32))

    def scan(i, carry):
        blk = [[i * unroll + u for u in range(unroll)], [n - 1 - (i * unroll + u) for u in range(unroll)]]
        rows = [[pl.ds(pl.multiple_of(blk[d][u] * c, c), c) for u in range(unroll)] for d in range(2)]
        sig, gcs, tots = {}, {}, {}
        for u in range(unroll):
            for d in range(2):
                gt = gt_ref[rows[d][u], :]
                sig[d, u] = 1.0 / (1.0 + jnp.exp(-gt))
                g3 = split3(-jnp.exp(arow_ref[...]) * _softplus(gt + dtrow_ref[...]))
                gcs[d, u] = mask_sum(incl_b[d], g3)
                tots[d, u] = mask_sum(same_b, g3)
        st = []
        for hh, d, u in probs:
            h = hb * hps + hh
            gidx = 2 * nheads + d * nheads + h
            beta = col(sig[d, u], d * nheads + h)
            gc = col(gcs[d, u], gidx)
            tot = col(tots[d, u], gidx)
            gcb = jnp.broadcast_to(gc, (c, c))
            diff = jnp.where(incl_d[d], gcb - jnp.transpose(gcb), 0.0)
            decay = jnp.where(incl_d[d], jnp.exp(diff), 0.0)
            ln = slice(hh * dk, (hh + 1) * dk)
            q = qn_ref[rows[d][u], ln]
            k = kn_ref[rows[d][u], ln]
            v = vn_ref[rows[d][u], ln]
            kb = k * beta
            kbf = k.astype(BF16)
            a = lax.dot_general(kb.astype(BF16), kbf, nt, preferred_element_type=F32)
            ab = jnp.where(strict_d[d], a * decay, 0.0).astype(BF16)
            attn = (lax.dot_general(q.astype(BF16), kbf, nt, preferred_element_type=F32) * decay).astype(BF16)
            rhs = jnp.concatenate([v * beta, kb * jnp.exp(gc)], axis=-1)
            st.append(dict(ab=ab, attn=attn, rhs=rhs, qg=(q * jnp.exp(gc)).astype(BF16),
                           kd=(k * jnp.exp(tot - gc)).astype(BF16), etot=jnp.exp(tot)))
        zero = jnp.zeros((c, c), BF16)
        n8 = [jnp.where(same8, p["ab"], zero) for p in st]
        n2 = [jnp.dot(m, m, preferred_element_type=F32) for m in n8]
        n2p = [_split_bf16(m) for m in n2]
        n4 = [_pdot(m, m) for m in n2p]
        inv = [_pdot(((eye - m.astype(F32)).astype(BF16),), _split_bf16(eye + m2)) for m, m2 in zip(n8, n2)]
        inv = [_pdot(_split_bf16(m), _split_bf16(eye + m4)) for m, m4 in zip(inv, n4)]
        for lv in range(3, sh):
            pair = jnp.logical_and(same_lv[lv + 1], jnp.logical_not(same_lv[lv]))
            invp = [_split_bf16(m) for m in inv]
            t = [_pdot(mp, (jnp.where(pair, p["ab"], zero),)) for mp, p in zip(invp, st)]
            inv = [m - _pdot(_split_bf16(tt), mp) for m, tt, mp in zip(inv, t, invp)]
        xs = [_pdot(_split_bf16(m), _split_bf16(p["rhs"])) for m, p in zip(inv, st)]
        for u in range(unroll):
            for t in range(nsub):
                for pi, (hh, d, uu) in enumerate(probs):
                    if uu != u:
                        continue
                    p = st[pi]
                    r0 = (t if d == 0 else nsub - 1 - t) * cc
                    s_ref = s_refs[2 * hh + d]
                    vnew_ref = vnew_refs[pi]
                    s = s_ref[...]
                    sb = s.astype(BF16)
                    x = xs[pi]
                    v_new = x[r0:r0 + cc, :dk] - jnp.dot(x[r0:r0 + cc, dk:].astype(BF16), sb,
                                                         preferred_element_type=F32)
                    vnb = v_new.astype(BF16)
                    vnew_ref[r0:r0 + cc, :] = vnb
                    o = (jnp.dot(p["qg"][r0:r0 + cc, :], sb, preferred_element_type=F32)
                         + jnp.dot(p["attn"][r0:r0 + cc, :], vnew_ref[...], preferred_element_type=F32))
                    out_ref = of_ref if d == 0 else ob_ref
                    out_ref[pl.ds(pl.multiple_of(blk[d][u] * c + r0, cc), cc), hh * dk:(hh + 1) * dk] = o
                    upd = lax.dot_general(p["kd"][r0:r0 + cc, :], vnb, tn, preferred_element_type=F32)
                    s_ref[...] = s * p["etot"][r0:r0 + 1, :] + upd
        return carry

    lax.fori_loop(0, n // unroll, scan, 0)

    for hh in range(hps):
        for d in range(2):
            sfin_ref[0, d, hh] = s_refs[2 * hh + d][...]

    def fin(i, carry):
        rows = pl.ds(pl.multiple_of(i * c, c), c)
        for hh in range(hps):
            ln = slice(hh * dk, (hh + 1) * dk)
            o = of_ref[rows, ln] + ob_ref[rows, ln]
            o = o * lax.rsqrt(jnp.mean(o * o, axis=-1, keepdims=True) + EPS) * ng_ref[...]
            o_ref[rows, ln] = (o * _silu(z_ref[rows, ln].astype(F32))).astype(o_ref.dtype)
        return carry

    lax.fori_loop(0, n, fin, 0)


def deltanet_core(qkv, z, gates, w_conv_t, a_row, dt_row, norm_g, s0, *, row0, nseq, seqlen, seg, nheads, hps,
                  unroll=1):
    m, d3 = qkv.shape
    d = d3 // 3
    dk = d // nheads
    assert dk == 128 and row0 % seqlen == 0 and nheads % hps == 0
    chunk = min(seqlen, 256)
    assert seqlen % (chunk * unroll) == 0 and chunk % seg == 0 and chunk % DN_CHUNK == 0
    rb = row0 // seqlen
    nhb = nheads // hps
    kw = w_conv_t.shape[0]
    g4 = gates.shape[1]
    wd = hps * dk

    def colspec(part):
        return pl.BlockSpec((seqlen, wd), lambda b, h: (rb + b, part * nhb + h))

    def wspec(part):
        return pl.BlockSpec((kw, wd), lambda b, h: (0, part * nhb + h))

    kern = functools.partial(_dn_kernel, seg=seg, chunk=chunk, nheads=nheads, hps=hps, unroll=unroll)
    st_spec = pl.BlockSpec((1, 2, hps, dk, dk), lambda b, h: (b, 0, h, 0, 0))
    out, sfin = pl.pallas_call(
        kern,
        grid=(nseq, nhb),
        in_specs=[colspec(0), colspec(1), colspec(2),
                  pl.BlockSpec((seqlen, wd), lambda b, h: (rb + b, h)),
                  pl.BlockSpec((seqlen, g4), lambda b, h: (rb + b, 0)),
                  wspec(0), wspec(1), wspec(2),
                  pl.BlockSpec((1, g4), lambda b, h: (0, 0)),
                  pl.BlockSpec((1, g4), lambda b, h: (0, 0)),
                  pl.BlockSpec((1, dk), lambda b, h: (0, 0)),
                  st_spec],
        out_specs=[pl.BlockSpec((seqlen, wd), lambda b, h: (b, h)), st_spec],
        out_shape=[jax.ShapeDtypeStruct((nseq * seqlen, d), BF16),
                   jax.ShapeDtypeStruct((nseq, 2, nheads, dk, dk), F32)],
        scratch_shapes=([pltpu.VMEM((seqlen, wd), F32) for _ in range(5)]
                        + [pltpu.VMEM((dk, dk), F32) for _ in range(2 * hps)]
                        + [pltpu.VMEM((chunk, dk), BF16) for _ in range(2 * hps * unroll)]),
        compiler_params=_cparams(("parallel", "parallel")),
        name="deltanet_core",
    )(qkv, qkv, qkv, z, gates, w_conv_t, w_conv_t, w_conv_t, a_row, dt_row, norm_g.reshape(1, dk), s0)
    return out, sfin


def _dot3(a_hi, a_lo, x):
    x_hi, x_lo = _split_bf16(x)
    acc = jnp.dot(a_hi, x_hi, preferred_element_type=F32)
    acc = acc + jnp.dot(a_hi, x_lo, preferred_element_type=F32)
    return acc + jnp.dot(a_lo, x_hi, preferred_element_type=F32)


def _dft_mats(p):
    n = 2 * p
    k = np.arange(p, dtype=np.float64)[:, None]
    s = np.arange(n, dtype=np.float64)[None, :]
    th = np.pi * (2.0 * k + 1.0) * s / n
    fwd = np.concatenate([np.cos(th), -np.sin(th)], axis=0)
    tht = th[:, :p].T
    inv = np.concatenate([np.cos(tht), -np.sin(tht)], axis=1) * (2.0 / n)
    return fwd, inv


def _hi_lo_const(a):
    a32 = np.asarray(a, np.float32)
    hi = a32.astype(BF16)
    lo = (a32 - hi.astype(np.float32)).astype(BF16)
    return jnp.asarray(hi), jnp.asarray(lo)


def _spec_kernel(ah_ref, al_ref, t_ref, o_ref):
    o_ref[0] = _dot3(ah_ref[...], al_ref[...], t_ref[0])


def filter_spectra(taps, p):
    nd, n, c = taps.shape
    fwd, _ = _dft_mats(p)
    fh, fl = _hi_lo_const(fwd)
    tn = _pick(c, (512, 256, 128))
    return pl.pallas_call(
        _spec_kernel,
        grid=(nd, c // tn),
        in_specs=[pl.BlockSpec((n, n), lambda d, j: (0, 0)),
                  pl.BlockSpec((n, n), lambda d, j: (0, 0)),
                  pl.BlockSpec((1, n, tn), lambda d, j: (d, 0, j))],
        out_specs=pl.BlockSpec((1, n, tn), lambda d, j: (d, 0, j)),
        out_shape=jax.ShapeDtypeStruct((nd, n, c), F32),
        compiler_params=_cparams(("parallel", "parallel")),
        name="hyena_filter_spectra",
    )(fh, fl, taps)


def _hy_kernel(x0_ref, x1_ref, v_ref, w0_ref, w1_ref, w2_ref, bias_ref, spec_ref,
               fh_ref, ih_ref, o_ref, vv_ref, vh_ref, acc_ref, *, seg, p, band):
    l, tc = v_ref.shape
    nb = l // p

    def fwd(j, carry):
        rows = pl.ds(pl.multiple_of(j * p, p), p)
        vv = _seg_conv(v_ref[rows, :], w2_ref, seg) * _seg_conv(x1_ref[rows, :], w1_ref, seg)
        vv_ref[rows, :] = vv
        vh_ref[j] = jnp.dot(fh_ref[...], vv.astype(BF16), preferred_element_type=F32)
        return carry

    lax.fori_loop(0, nb, fwd, 0)

    rt = 64

    def out_block(i, carry):
        def row_tile(r, c2):
            rr = pl.ds(pl.multiple_of(r * rt, rt), rt)
            ri = pl.ds(pl.multiple_of(p + r * rt, rt), rt)

            j0 = (i // band) * band

            def mac(jj, acc):
                are, aim = acc
                j = j0 + jj
                d = i - j + (band - 1)
                fre = spec_ref[d, rr, :]
                fim = spec_ref[d, ri, :]
                xre = vh_ref[j, rr, :]
                xim = vh_ref[j, ri, :]
                return (are + fre * xre - fim * xim, aim + fre * xim + fim * xre)

            zero = jnp.zeros((rt, tc), F32)
            are, aim = lax.fori_loop(0, band, mac, (zero, zero))
            acc_ref[rr, :] = are
            acc_ref[ri, :] = aim
            return c2

        lax.fori_loop(0, p // rt, row_tile, 0)
        y = jnp.dot(ih_ref[...], acc_ref[...].astype(BF16), preferred_element_type=F32)
        rows = pl.ds(pl.multiple_of(i * p, p), p)
        y = y + vv_ref[rows, :] * bias_ref[...]
        o_ref[rows, :] = (y * _seg_conv(x0_ref[rows, :], w0_ref, seg)).astype(o_ref.dtype)
        return carry

    lax.fori_loop(0, nb, out_block, 0)


def hyena_core(u, w_conv_t, bias, spec, *, row0, nseq, seqlen, seg, p, spb=1):
    m, c3 = u.shape
    c = c3 // 3
    tc = 128
    assert c % tc == 0 and seqlen % p == 0 and p % seg == 0 and nseq % spb == 0
    band = seqlen // p
    assert spec.shape[0] == 2 * band - 1
    nseq, seqlen = nseq // spb, seqlen * spb
    assert row0 % seqlen == 0
    nct = c // tc
    rb = row0 // seqlen
    nd = spec.shape[0]
    kw = w_conv_t.shape[0]
    fwd, inv = _dft_mats(p)
    fh, _ = _hi_lo_const(fwd[:, :p])
    ih, _ = _hi_lo_const(inv)

    def colspec(part):
        return pl.BlockSpec((seqlen, tc), lambda j, b: (rb + b, part * nct + j))

    def wspec(part):
        return pl.BlockSpec((kw, tc), lambda j, b: (0, part * nct + j))

    const = lambda shape: pl.BlockSpec(shape, lambda j, b: (0,) * len(shape))
    kern = functools.partial(_hy_kernel, seg=seg, p=p, band=band)
    return pl.pallas_call(
        kern,
        grid=(nct, nseq),
        in_specs=[colspec(0), colspec(1), colspec(2), wspec(0), wspec(1), wspec(2),
                  pl.BlockSpec((1, tc), lambda j, b: (0, j)),
                  pl.BlockSpec((nd, 2 * p, tc), lambda j, b: (0, 0, j)),
                  const((2 * p, p)), const((p, 2 * p))],
        out_specs=pl.BlockSpec((seqlen, tc), lambda j, b: (b, j)),
        out_shape=jax.ShapeDtypeStruct((nseq * seqlen, c), BF16),
        scratch_shapes=[pltpu.VMEM((seqlen, tc), F32),
                        pltpu.VMEM((seqlen // p, 2 * p, tc), F32),
                        pltpu.VMEM((2 * p, tc), F32)],
        compiler_params=_cparams(("parallel", "parallel")),
        name="hyena_core",
    )(u, u, u, w_conv_t, w_conv_t, w_conv_t, bias.reshape(1, c), spec, fh, ih)


def _filter_kernel(feat_ref, w1_ref, b1_ref, w2_ref, b2_ref, f_ref, w3_ref, dec_ref, o_ref):
    feat = feat_ref[...]
    hid = jnp.sin(f_ref[0:1, :] * (jnp.dot(feat, w1_ref[...], precision=HI, preferred_element_type=F32)
                                  + b1_ref[...]))
    hid = jnp.sin(f_ref[1:2, :] * (jnp.dot(hid, w2_ref[...], precision=HI, preferred_element_type=F32)
                                  + b2_ref[...]))
    h = jnp.dot(hid, w3_ref[...], precision=HI, preferred_element_type=F32)
    o_ref[...] = h * jnp.exp(-feat[:, 0:1] * jnp.abs(dec_ref[...]))


def hyena_filter_taps(l, p, w1, b1, w2, b2, w3, freq, decay):
    t = np.linspace(0.0, 1.0, l, dtype=np.float32)[:, None]
    bands = np.linspace(1e-4, HY_BANDS - 1, HY_BANDS, dtype=np.float32)
    ang = (np.float32(2.0 * math.pi / l) * np.arange(l, dtype=np.float32)[:, None] * bands).astype(np.float32)
    feat = np.concatenate([t, np.cos(ang), -np.sin(ang)], axis=-1).astype(np.float32)
    kf = 128
    nfeat, hidden = w1.shape
    c2 = w3.shape[1]
    feat = jnp.asarray(np.pad(feat, ((0, 0), (0, kf - nfeat))))
    w1p = jnp.pad(w1, ((0, kf - nfeat), (0, 0)))
    tcn = _pick(c2, (1024, 512, 256, 128))
    const = lambda shape: pl.BlockSpec(shape, lambda j: (0,) * len(shape))
    h = pl.pallas_call(
        _filter_kernel,
        grid=(c2 // tcn,),
        in_specs=[const((l, kf)), const((kf, hidden)), const((1, hidden)), const((hidden, hidden)),
                  const((1, hidden)), const((2, hidden)),
                  pl.BlockSpec((hidden, tcn), lambda j: (0, j)),
                  pl.BlockSpec((1, tcn), lambda j: (0, j))],
        out_specs=pl.BlockSpec((l, tcn), lambda j: (0, j)),
        out_shape=jax.ShapeDtypeStruct((l, c2), F32),
        compiler_params=_cparams(("parallel",)),
        name="hyena_filter",
    )(feat, w1p, b1[None, :], w2, b2[None, :], freq, w3, decay.reshape(1, c2))
    h_fwd, h_bwd = h[:, :c2 // 2], h[:, c2 // 2:]
    denom = (jnp.sum(jnp.abs(h_fwd), axis=0, keepdims=True)
             + jnp.sum(jnp.abs(h_bwd[1:]), axis=0, keepdims=True) + EPS)
    hcat = jnp.concatenate([h_fwd, h_bwd], axis=0) / denom
    nb = l // p
    n = 2 * p
    mi = np.arange(n)
    delta = np.where(mi < p, mi, mi - n)
    sign = np.where(mi < p, 1.0, -1.0) * (mi != p)
    off = np.arange(-(nb - 1), nb)[:, None] * p + delta[None, :]
    valid = np.abs(off) < l
    idx = np.where(valid, np.where(off >= 0, off, l - off), 0).astype(np.int32)
    coef = (valid * sign[None, :]).astype(np.float32)
    return hcat[jnp.asarray(idx)] * jnp.asarray(coef)[:, :, None]


def _gather_kernel(idx_ref, src_ref, o_ref, stage_ref, sem):
    i = pl.program_id(0)
    tg = stage_ref.shape[0]

    def issue(r2, carry):
        for pr in range(2):
            r = 2 * r2 + pr
            row = idx_ref[i * tg + r]
            pltpu.make_async_copy(src_ref.at[pl.ds(row, 1)], stage_ref.at[pl.ds(r, 1)], sem).start(priority=pr)
        return carry

    lax.fori_loop(0, tg // 2, issue, 0)

    def drain(r, carry):
        pltpu.make_async_copy(src_ref.at[pl.ds(0, 1)], stage_ref.at[pl.ds(r, 1)], sem).wait()
        return carry

    lax.fori_loop(0, tg, drain, 0)
    o_ref[...] = stage_ref[...].astype(o_ref.dtype)


def gather_rows(src, idx, tg=512):
    npick = idx.shape[0]
    d = src.shape[1]
    assert npick % tg == 0
    return pl.pallas_call(
        _gather_kernel,
        grid_spec=pltpu.PrefetchScalarGridSpec(
            num_scalar_prefetch=1,
            grid=(npick // tg,),
            in_specs=[pl.BlockSpec(memory_space=pl.ANY)],
            out_specs=pl.BlockSpec((tg, d), lambda i, idx_ref: (i, 0)),
            scratch_shapes=[pltpu.VMEM((tg, d), F32), pltpu.SemaphoreType.DMA(())],
        ),
        out_shape=jax.ShapeDtypeStruct((npick, d), BF16),
        compiler_params=_cparams(("arbitrary",)),
        name="moe_gather",
    )(idx, src)


def _ffn_kernel(xs_ref, wg_ref, wu_ref, wd_ref, o_ref, hid_ref, *, nf, tf):
    j = pl.program_id(1)

    @pl.when(j < nf)
    def _():
        xs = xs_ref[...]
        g = jnp.dot(xs, wg_ref[0].astype(BF16), preferred_element_type=F32)
        u = jnp.dot(xs, wu_ref[0].astype(BF16), preferred_element_type=F32)
        hid_ref[j] = (_silu(g) * u).astype(BF16)

    @pl.when(j >= nf)
    def _():
        acc = jnp.dot(hid_ref[0], wd_ref[0, 0:tf, :].astype(BF16), preferred_element_type=F32)
        for f in range(1, nf):
            acc = acc + jnp.dot(hid_ref[f], wd_ref[0, f * tf:(f + 1) * tf, :].astype(BF16),
                                preferred_element_type=F32)
        o_ref[...] = acc


def expert_ffn(xs, w_gate, w_up, w_down, lyr, cap):
    npick, d = xs.shape
    _, e, _, ff = w_gate.shape
    nse = npick // cap
    tf = _pick(ff, (256, 128))
    td = _pick(d, (512, 256, 128))
    nf, ndt = ff // tf, d // td
    kern = functools.partial(_ffn_kernel, nf=nf, tf=tf)
    return pl.pallas_call(
        kern,
        grid=(nse, nf + ndt),
        in_specs=[pl.BlockSpec((cap, d), lambda s, j: (s, 0)),
                  pl.BlockSpec((None, 1, d, tf), lambda s, j: (lyr, s % e, 0, jnp.minimum(j, nf - 1))),
                  pl.BlockSpec((None, 1, d, tf), lambda s, j: (lyr, s % e, 0, jnp.minimum(j, nf - 1))),
                  pl.BlockSpec((None, 1, ff, td), lambda s, j: (lyr, s % e, 0, jnp.maximum(j - nf, 0)))],
        out_specs=pl.BlockSpec((cap, td), lambda s, j: (s, jnp.maximum(j - nf, 0))),
        out_shape=jax.ShapeDtypeStruct((npick, d), F32),
        scratch_shapes=[pltpu.VMEM((nf, cap, tf), BF16)],
        compiler_params=_cparams(("parallel", "arbitrary")),
        name="moe_expert_ffn",
    )(xs, w_gate, w_up, w_down)


def _scatter_kernel(idx_ref, ye_ref, coef_ref, sel_ref, ga_ref, gb_ref, x_in_ref, x_ref,
                    stage_ref, sem_r, sem_w):
    del x_in_ref
    i = pl.program_id(0)
    tg = stage_ref.shape[0]

    def rd(r2, carry):
        for pr in range(2):
            r = 2 * r2 + pr
            row = idx_ref[i * tg + r]
            pltpu.make_async_copy(x_ref.at[pl.ds(row, 1)], stage_ref.at[pl.ds(r, 1)], sem_r).start(priority=pr)
        return carry

    lax.fori_loop(0, tg // 2, rd, 0)

    def rd_wait(r, carry):
        pltpu.make_async_copy(x_ref.at[pl.ds(0, 1)], stage_ref.at[pl.ds(r, 1)], sem_r).wait()
        return carry

    lax.fori_loop(0, tg, rd_wait, 0)
    gvec = jnp.where(sel_ref[...] > 0.5, gb_ref[0], ga_ref[0])
    stage_ref[...] = stage_ref[...] + gvec * (coef_ref[...] * ye_ref[...])

    def wr(r2, carry):
        for pr in range(2):
            r = 2 * r2 + pr
            row = idx_ref[i * tg + r]
            pltpu.make_async_copy(stage_ref.at[pl.ds(r, 1)], x_ref.at[pl.ds(row, 1)], sem_w).start(priority=pr)
        return carry

    lax.fori_loop(0, tg // 2, wr, 0)

    def wr_wait(r, carry):
        pltpu.make_async_copy(stage_ref.at[pl.ds(r, 1)], x_ref.at[pl.ds(0, 1)], sem_w).wait()
        return carry

    lax.fori_loop(0, tg, wr_wait, 0)


def scatter_add_rows(x, ye, idx, coef, sel, gab, picks_per_stream, tg=512):
    npick, d = ye.shape
    assert npick % tg == 0 and picks_per_stream % tg == 0
    cps = picks_per_stream // tg
    return pl.pallas_call(
        _scatter_kernel,
        grid_spec=pltpu.PrefetchScalarGridSpec(
            num_scalar_prefetch=1,
            grid=(npick // tg,),
            in_specs=[pl.BlockSpec((tg, d), lambda i, idx_ref: (i, 0)),
                      pl.BlockSpec((tg, 1), lambda i, idx_ref: (i, 0)),
                      pl.BlockSpec((tg, 1), lambda i, idx_ref: (i, 0)),
                      pl.BlockSpec((1, 1, d), lambda i, idx_ref: (2 * (i // cps), 0, 0)),
                      pl.BlockSpec((1, 1, d), lambda i, idx_ref: (2 * (i // cps) + 1, 0, 0)),
                      pl.BlockSpec(memory_space=pl.ANY)],
            out_specs=pl.BlockSpec(memory_space=pl.ANY),
            scratch_shapes=[pltpu.VMEM((tg, d), F32), pltpu.SemaphoreType.DMA(()),
                            pltpu.SemaphoreType.DMA(())],
        ),
        out_shape=jax.ShapeDtypeStruct(x.shape, x.dtype),
        input_output_aliases={6: 0},
        compiler_params=_cparams(("arbitrary",)),
        name="moe_scatter_add",
    )(idx, ye, coef, sel, gab, gab, x)


def moe_layer(x, hn, aff, g2, w_gate, w_up, w_down, lyr, n_ctx, dec_seq):
    m, d = x.shape
    e = aff.shape[1]
    n_dec = m - n_ctx
    assert n_ctx == n_dec, "merged MoE kernels need equal routed-set sizes"
    cap = EC_FACTOR * n_ctx // e
    gate0, idx0 = lax.top_k(aff[:n_ctx].T, cap)
    gate1, idx1 = lax.top_k(aff[n_ctx:].T, cap)
    idx = jnp.concatenate([idx0.reshape(-1), idx1.reshape(-1) + n_ctx]).astype(jnp.int32)
    coef = jnp.concatenate([gate0.reshape(-1), gate1.reshape(-1)])[:, None]
    sel = (idx >= n_ctx + dec_seq).astype(F32)[:, None]
    ngrp = g2.shape[0]
    gab = jnp.stack([g2[0], g2[0], g2[1], g2[min(2, ngrp - 1)]])[:, None, :]
    xs = gather_rows(hn, idx)
    ye = expert_ffn(xs, w_gate, w_up, w_down, lyr, cap)
    return scatter_add_rows(x, ye, idx, coef, sel, gab, e * cap)


def kernel(x_prompt, x_sample, state_dn, c, c_ctx, ada_w, ada_b, norm1_g, norm2_g, dn_w_in, dn_conv, dn_a_log, dn_dt_bias, dn_norm_g, dn_w_out, hy_w_in, hy_conv, hy_w1, hy_b1, hy_w2, hy_b2, hy_w3, hy_freq, hy_decay, hy_bias, hy_w_out, moe_router, moe_w_gate, moe_w_up, moe_w_down, final_g):
    bsz, seq, d = x_prompt.shape
    dbsz, dseq, _ = x_sample.shape
    assert dbsz == 2, "scatter group select handles two latent requests"
    depth = ada_w.shape[0]
    nheads = dn_a_log.shape[2]
    n_ctx = bsz * seq
    n_dec = dbsz * dseq
    x = jnp.concatenate([x_prompt.reshape(n_ctx, d), x_sample.reshape(n_dec, d)], axis=0)

    cvec = jnp.concatenate([c_ctx[None, :], c], axis=0)
    ngrp = cvec.shape[0]
    sc_in = jnp.zeros((16, d), F32).at[:ngrp].set(cvec)
    sc_in = (sc_in * (1.0 / (1.0 + jnp.exp(-sc_in)))).astype(BF16)

    p_ctx = min(seq, 512)
    p_dec = min(dseq, 512)
    states = []
    for i in range(depth):
        mod = matmul(sc_in, ada_w, widx=i, bias=ada_b[i], tm=16)[:ngrp]
        sh1, sc1, g1, sh2, sc2, g2 = [mod[:, t * d:(t + 1) * d] for t in range(N_MOD)]
        hn = norm_mod(x, norm1_g[i], sc1, sh1, n_ctx, dseq, BF16)
        j = i // 2
        if i % 2 == 0:
            qkv = matmul(hn, dn_w_in, widx=j, col0=0, ncols=3 * d)
            z = matmul(hn, dn_w_in, widx=j, col0=3 * d, ncols=d, out_dtype=BF16)
            gates = matmul(hn, dn_w_in, widx=j, col0=4 * d, ncols=4 * nheads, tn=4 * nheads)
            wct = dn_conv[j].T
            a_row = jnp.concatenate([jnp.zeros((2 * nheads,), F32), dn_a_log[j].reshape(-1)])[None, :]
            dt_row = jnp.concatenate([jnp.zeros((2 * nheads,), F32), dn_dt_bias[j].reshape(-1)])[None, :]
            dk = d // nheads
            o_ctx, s_ctx = deltanet_core(qkv, z, gates, wct, a_row, dt_row, dn_norm_g[j],
                                         jnp.zeros((bsz, 2, nheads, dk, dk), F32),
                                         row0=0, nseq=bsz, seqlen=seq, seg=seq, nheads=nheads, hps=4)
            o_dec, _ = deltanet_core(qkv, z, gates, wct, a_row, dt_row, dn_norm_g[j], state_dn[:, j],
                                     row0=n_ctx, nseq=dbsz, seqlen=dseq, seg=GRID_W, nheads=nheads, hps=1,
                                     unroll=2)
            states.append(s_ctx)
            x = matmul(o_ctx, dn_w_out, a2=o_dec, widx=j, resid=(x, g1, n_ctx, dseq))
        else:
            u = matmul(hn, hy_w_in, widx=j)
            wct = hy_conv[j].T
            filt = (hy_w1[j], hy_b1[j], hy_w2[j], hy_b2[j], hy_w3[j], hy_freq[j], hy_decay[j])
            spec_ctx = filter_spectra(hyena_filter_taps(seq, p_ctx, *filt), p_ctx)
            spec_dec = filter_spectra(hyena_filter_taps(dseq, p_dec, *filt), p_dec)
            y_ctx = hyena_core(u, wct, hy_bias[j], spec_ctx, row0=0, nseq=bsz, seqlen=seq, seg=seq, p=p_ctx,
                               spb=_pick(bsz, (8, 4, 2)))
            y_dec = hyena_core(u, wct, hy_bias[j], spec_dec, row0=n_ctx, nseq=dbsz, seqlen=dseq,
                               seg=GRID_W, p=p_dec)
            x = matmul(y_ctx, hy_w_out, a2=y_dec, widx=j, resid=(x, g1, n_ctx, dseq))
        hn2, aff = norm_mod(x, norm2_g[i], sc2, sh2, n_ctx, dseq, F32, w_router=moe_router[i])
        x = moe_layer(x, hn2, aff, g2, moe_w_gate, moe_w_up, moe_w_down, i, n_ctx, dseq)
    y = final_norm(x, final_g)
    y_prompt = y[:n_ctx].reshape(bsz, seq, d)
    y_sample = y[n_ctx:].reshape(dbsz, dseq, d)
    return (y_prompt, y_sample, jnp.stack(states, axis=1))
```
